```python
import math
import jax, jax.numpy as jnp
from jax import lax
import numpy as np

D_MODEL = 2048
BATCH = 4
SEQ = 4096
DEPTH = 2

CHUNK = 64
Q_BLOCK = 128
MIX_WIDTH = D_MODEL
DIFF_WIDTH = MIX_WIDTH // 2
MLA_WIDTH = MIX_WIDTH - DIFF_WIDTH
DIFF_HEAD_DIM = 64
DIFF_HEADS = DIFF_WIDTH // (2 * DIFF_HEAD_DIM)
MLA_V_DIM = 128
MLA_HEADS = MLA_WIDTH // MLA_V_DIM
MLA_NOPE = 128
MLA_ROPE = 64
MLA_Q_LORA = 512
MLA_KV_LORA = 256
ROPE_BASE = 10000.0
REL_BUCKETS = 32
REL_MAX_DIST = 128
N_BIAS_HEADS = 2 * DIFF_HEADS
EPS = 1e-6
NEG = -1e30
SPLIT_SIZES = (
    DIFF_HEADS * 2 * DIFF_HEAD_DIM,
    DIFF_HEADS * 2 * DIFF_HEAD_DIM,
    DIFF_HEADS * 2 * DIFF_HEAD_DIM,
    MLA_Q_LORA,
    MLA_KV_LORA,
    MLA_ROPE,
    MIX_WIDTH,
)
IN_WIDTH = sum(SPLIT_SIZES)

kernel_name = 'hybrid_diffattn_mla_chunk_causal'


def _rmsnorm(x, g):
    xf = x.astype(jnp.float32)
    y = xf * lax.rsqrt(jnp.mean(xf * xf, axis=-1, keepdims=True) + EPS)
    return (y * g.astype(jnp.float32)).astype(x.dtype)


def _rel_bucket(rel):
    nb = REL_BUCKETS // 2
    max_exact = nb // 2
    ret = (rel > 0).astype(jnp.int32) * nb
    n = jnp.abs(rel)
    nf = jnp.maximum(n, 1).astype(jnp.float32)
    large = max_exact + (jnp.log(nf / max_exact) / math.log(REL_MAX_DIST / max_exact)
                         * (nb - max_exact)).astype(jnp.int32)
    large = jnp.minimum(large, nb - 1)
    return ret + jnp.where(n < max_exact, n, large)


def _chunk_mask(qs, ke):
    qpos = jnp.arange(qs, qs + Q_BLOCK)
    kpos = jnp.arange(ke)
    allowed = (kpos[None, :] // CHUNK) <= (qpos[:, None] // CHUNK)
    return qpos, kpos, allowed


def _rope(x, cos, sin):
    half = x.shape[-1] // 2
    x1 = x[..., :half].astype(jnp.float32)
    x2 = x[..., half:].astype(jnp.float32)
    return jnp.concatenate([x1 * cos - x2 * sin, x1 * sin + x2 * cos], axis=-1).astype(x.dtype)


def _diff_attention(q1, q2, k1, k2, v, lam, rel_bias):
    S = q1.shape[2]
    scale = DIFF_HEAD_DIM ** -0.5
    outs = []
    for qs in range(0, S, Q_BLOCK):
        ke = qs + Q_BLOCK
        qpos, kpos, allowed = _chunk_mask(qs, ke)
        bias = rel_bias[_rel_bucket(kpos[None, :] - qpos[:, None])].astype(jnp.float32)
        bias = bias.transpose(2, 0, 1).reshape(DIFF_HEADS, 2, Q_BLOCK, ke)

        def probs(qm, km, m):
            s = jnp.einsum('bhqd,bhkd->bhqk', qm[:, :, qs:ke], km[:, :, :ke]).astype(jnp.float32)
            s = s * scale + bias[:, m]
            return jax.nn.softmax(jnp.where(allowed, s, NEG), axis=-1)

        p = probs(q1, k1, 0) - lam * probs(q2, k2, 1)
        outs.append(jnp.einsum('bhqk,bhkd->bhqd', p.astype(v.dtype), v[:, :, :ke]))
    return jnp.concatenate(outs, axis=2)


def _mla_attention(qn, qr, kn, kr, v):
    S = qn.shape[2]
    scale = (MLA_NOPE + MLA_ROPE) ** -0.5
    outs = []
    for qs in range(0, S, Q_BLOCK):
        ke = qs + Q_BLOCK
        _, _, allowed = _chunk_mask(qs, ke)
        s = (jnp.einsum('bhqd,bhkd->bhqk', qn[:, :, qs:ke], kn[:, :, :ke])
             + jnp.einsum('bhqr,bkr->bhqk', qr[:, :, qs:ke], kr[:, :ke])).astype(jnp.float32) * scale
        p = jax.nn.softmax(jnp.where(allowed, s, NEG), axis=-1)
        outs.append(jnp.einsum('bhqk,bhkd->bhqd', p.astype(v.dtype), v[:, :, :ke]))
    return jnp.concatenate(outs, axis=2)


def setup_inputs(seed: int = 0) -> dict:
    key = jax.random.key(seed)
    ks = jax.random.split(key, 12)

    def nrm(k, shape, s):
        return jax.random.normal(k, shape, jnp.float32) * s

    return {
        'x': nrm(ks[0], (BATCH, SEQ, D_MODEL), 1.0),
        'norm_g': 1.0 + nrm(ks[1], (DEPTH, D_MODEL), 0.02),
        'w_in': nrm(ks[2], (DEPTH, D_MODEL, IN_WIDTH), D_MODEL ** -0.5),
        'diff_lambda': nrm(ks[3], (DEPTH, 4, DIFF_HEAD_DIM), 0.1),
        'diff_subln_g': 1.0 + nrm(ks[4], (DEPTH, 2 * DIFF_HEAD_DIM), 0.02),
        'mla_q_norm_g': 1.0 + nrm(ks[5], (DEPTH, MLA_Q_LORA), 0.02),
        'mla_w_q_b': nrm(ks[6], (DEPTH, MLA_Q_LORA, MLA_HEADS * (MLA_NOPE + MLA_ROPE)), MLA_Q_LORA ** -0.5),
        'mla_kv_norm_g': 1.0 + nrm(ks[7], (DEPTH, MLA_KV_LORA), 0.02),
        'mla_w_kv_b': nrm(ks[8], (DEPTH, MLA_KV_LORA, MLA_HEADS * (MLA_NOPE + MLA_V_DIM)), MLA_KV_LORA ** -0.5),
        'w_out': nrm(ks[9], (DEPTH, MIX_WIDTH, D_MODEL), MIX_WIDTH ** -0.5),
        'rel_bias': nrm(ks[10], (REL_BUCKETS, N_BIAS_HEADS), 0.2),
        'final_norm_g': 1.0 + nrm(ks[11], (D_MODEL,), 0.02),
    }


def reference(x, norm_g, w_in, diff_lambda, diff_subln_g, mla_q_norm_g, mla_w_q_b,
              mla_kv_norm_g, mla_w_kv_b, w_out, rel_bias, final_norm_g):
    B, S, _ = x.shape
    offs = [sum(SPLIT_SIZES[:i + 1]) for i in range(len(SPLIT_SIZES) - 1)]

    pos = jnp.arange(S, dtype=jnp.float32)
    inv_freq = ROPE_BASE ** (-jnp.arange(0, MLA_ROPE, 2, dtype=jnp.float32) / MLA_ROPE)
    ang = pos[:, None] * inv_freq[None, :]
    cos, sin = jnp.cos(ang), jnp.sin(ang)

    for l in range(DEPTH):
        h = _rmsnorm(x, norm_g[l])
        proj = jnp.einsum('bsd,de->bse', h, w_in[l])
        dq, dk, dv, cq, ckv, kr, gate = jnp.split(proj, offs, axis=-1)

        dq = dq.reshape(B, S, DIFF_HEADS, 2, DIFF_HEAD_DIM).transpose(0, 2, 3, 1, 4)
        dk = dk.reshape(B, S, DIFF_HEADS, 2, DIFF_HEAD_DIM).transpose(0, 2, 3, 1, 4)
        dv = dv.reshape(B, S, DIFF_HEADS, 2 * DIFF_HEAD_DIM).transpose(0, 2, 1, 3)
        lam_init = 0.8 - 0.6 * math.exp(-0.3 * l)
        lp = diff_lambda[l].astype(jnp.float32)
        lam = jnp.exp(jnp.sum(lp[0] * lp[1])) - jnp.exp(jnp.sum(lp[2] * lp[3])) + lam_init
        o_a = _diff_attention(dq[:, :, 0], dq[:, :, 1], dk[:, :, 0], dk[:, :, 1], dv, lam, rel_bias)
        o_a = _rmsnorm(o_a, diff_subln_g[l]) * (1.0 - lam_init)
        o_a = o_a.transpose(0, 2, 1, 3).reshape(B, S, DIFF_WIDTH)

        q = jnp.einsum('bsr,re->bse', _rmsnorm(cq, mla_q_norm_g[l]), mla_w_q_b[l])
        q = q.reshape(B, S, MLA_HEADS, MLA_NOPE + MLA_ROPE)
        q_nope = q[..., :MLA_NOPE].transpose(0, 2, 1, 3)
        q_rope = _rope(q[..., MLA_NOPE:], cos[None, :, None, :], sin[None, :, None, :]).transpose(0, 2, 1, 3)
        kv = jnp.einsum('bsr,re->bse', _rmsnorm(ckv, mla_kv_norm_g[l]), mla_w_kv_b[l])
        kv = kv.reshape(B, S, MLA_HEADS, MLA_NOPE + MLA_V_DIM)
        k_nope = kv[..., :MLA_NOPE].transpose(0, 2, 1, 3)
        v_b = kv[..., MLA_NOPE:].transpose(0, 2, 1, 3)
        k_rope = _rope(kr, cos[None], sin[None])
        o_b = _mla_attention(q_nope, q_rope, k_nope, k_rope, v_b)
        o_b = o_b.transpose(0, 2, 1, 3).reshape(B, S, MLA_WIDTH)

        y = jnp.concatenate([o_a, o_b], axis=-1) * jax.nn.silu(gate)
        x = x + jnp.einsum('bse,ed->bsd', y, w_out[l])

    return _rmsnorm(x, final_norm_g)
```

```python
import functools
import math

import jax
import jax.numpy as jnp
from jax import lax
from jax.experimental import pallas as pl
from jax.experimental.pallas import tpu as pltpu

D_MODEL = 2048
DEPTH = 2
CHUNK = 64
MIX_WIDTH = D_MODEL
DIFF_WIDTH = MIX_WIDTH // 2
MLA_WIDTH = MIX_WIDTH - DIFF_WIDTH
DIFF_HEAD_DIM = 64
DIFF_HEADS = DIFF_WIDTH // (2 * DIFF_HEAD_DIM)
MLA_V_DIM = 128
MLA_HEADS = MLA_WIDTH // MLA_V_DIM
MLA_NOPE = 128
MLA_ROPE = 64
MLA_Q_LORA = 512
MLA_KV_LORA = 256
ROPE_BASE = 10000.0
REL_BUCKETS = 32
REL_MAX_DIST = 128
EPS = 1e-6
NEG = -1e30
QKV_WIDTH = 3 * DIFF_WIDTH
LAT_OFF = QKV_WIDTH
GATE_OFF = LAT_OFF + MLA_Q_LORA + MLA_KV_LORA + MLA_ROPE
MAIN_WIDTH = QKV_WIDTH + MIX_WIDTH

LOG2E = math.log2(math.e)
LANE = 128
VMEM_LIMIT = 56 * 1024 * 1024

BF16 = jnp.bfloat16
F32 = jnp.float32

TM_PROJ = 1024
TN_PROJ = 1024
RC_NORM = 256
TM_LAT = 512
TM_OUT = 512
TQ_DIFF = 256
TQ_MLA = 512

_NT = (((1,), (1,)), ((), ()))
_TN = (((0,), (0,)), ((), ()))


def _rms_scale(xf):
    return lax.rsqrt(jnp.mean(xf * xf, axis=-1, keepdims=True) + EPS)


def _in_proj_kernel(x_ref, g_ref, w_ref, cs_ref, o_ref, h_ref, *, n_plain):
    j = pl.program_id(1)

    @pl.when(j == 0)
    def _():
        def body(c, carry):
            rows = pl.ds(pl.multiple_of(c * RC_NORM, RC_NORM), RC_NORM)
            xf = x_ref[rows, :]
            h_ref[rows, :] = (xf * _rms_scale(xf) * g_ref[...]).astype(BF16)
            return carry
        lax.fori_loop(0, TM_PROJ // RC_NORM, body, 0)

    acc = jnp.dot(h_ref[...], w_ref[...], preferred_element_type=F32) * cs_ref[...]

    @pl.when(j < n_plain)
    def _():
        o_ref[...] = acc.astype(BF16)

    @pl.when(j >= n_plain)
    def _():
        o_ref[...] = (acc * jax.nn.sigmoid(acc)).astype(BF16)


def _in_proj(x2, g, w_main, colscale):
    m = x2.shape[0]
    n = w_main.shape[1]
    return pl.pallas_call(
        functools.partial(_in_proj_kernel, n_plain=QKV_WIDTH // TN_PROJ),
        grid=(m // TM_PROJ, n // TN_PROJ),
        in_specs=[
            pl.BlockSpec((TM_PROJ, D_MODEL), lambda i, j: (i, 0)),
            pl.BlockSpec((1, D_MODEL), lambda i, j: (0, 0)),
            pl.BlockSpec((D_MODEL, TN_PROJ), lambda i, j: (0, j)),
            pl.BlockSpec((1, TN_PROJ), lambda i, j: (0, j)),
        ],
        out_specs=pl.BlockSpec((TM_PROJ, TN_PROJ), lambda i, j: (i, j)),
        out_shape=jax.ShapeDtypeStruct((m, n), BF16),
        scratch_shapes=[pltpu.VMEM((TM_PROJ, D_MODEL), BF16)],
        compiler_params=pltpu.CompilerParams(
            dimension_semantics=("parallel", "arbitrary"), vmem_limit_bytes=VMEM_LIMIT),
        name="in_proj",
    )(x2, g, w_main, colscale)


def _latent_kernel(x_ref, g_ref, wl_ref, gq_ref, gkv_ref, wq_ref, wkv_ref, cos_ref, sin_ref,
                   qn_ref, qr_ref, kv_ref, kr_ref, *, q_scale):
    xf = x_ref[...]
    h = (xf * _rms_scale(xf) * g_ref[...]).astype(BF16)
    lat = jnp.dot(h, wl_ref[...], preferred_element_type=F32)
    cq = lat[:, :MLA_Q_LORA]
    ckv = lat[:, MLA_Q_LORA:MLA_Q_LORA + MLA_KV_LORA]
    ka = lat[:, MLA_Q_LORA + MLA_KV_LORA:MLA_Q_LORA + MLA_KV_LORA + LANE]
    kb = lat[:, MLA_Q_LORA + MLA_KV_LORA + LANE:]
    cos = cos_ref[...]
    sin = sin_ref[...]

    cqn = (cq * _rms_scale(cq) * gq_ref[...]).astype(BF16)
    qall = jnp.dot(cqn, wq_ref[...], preferred_element_type=F32) * q_scale
    nope_w = MLA_HEADS * MLA_NOPE
    qn_ref[...] = qall[:, :nope_w].astype(BF16)
    for hh in range(MLA_HEADS):
        a = qall[:, nope_w + hh * LANE:nope_w + (hh + 1) * LANE]
        b = qall[:, 2 * nope_w + hh * LANE:2 * nope_w + (hh + 1) * LANE]
        qr_ref[:, hh * LANE:(hh + 1) * LANE] = (a * cos + b * sin).astype(BF16)

    ckvn = (ckv * _rms_scale(ckv) * gkv_ref[...]).astype(BF16)
    kv_ref[...] = jnp.dot(ckvn, wkv_ref[...], preferred_element_type=F32).astype(BF16)
    kr_ref[...] = (ka * cos + kb * sin).astype(BF16)


def _latent(x2, g, w_lat, gq, gkv, w_q2, w_kv2, cos_t, sin_t, seq):
    m = x2.shape[0]
    seq_tiles = seq // TM_LAT
    const = lambda i: (0, 0)
    row = lambda i: (i, 0)
    pos = lambda i: (i % seq_tiles, 0)
    q_scale = (MLA_NOPE + MLA_ROPE) ** -0.5 * LOG2E
    return pl.pallas_call(
        functools.partial(_latent_kernel, q_scale=q_scale),
        grid=(m // TM_LAT,),
        in_specs=[
            pl.BlockSpec((TM_LAT, D_MODEL), row),
            pl.BlockSpec((1, D_MODEL), const),
            pl.BlockSpec(w_lat.shape, const),
            pl.BlockSpec((1, MLA_Q_LORA), const),
            pl.BlockSpec((1, MLA_KV_LORA), const),
            pl.BlockSpec(w_q2.shape, const),
            pl.BlockSpec(w_kv2.shape, const),
            pl.BlockSpec((TM_LAT, LANE), pos),
            pl.BlockSpec((TM_LAT, LANE), pos),
        ],
        out_specs=[
            pl.BlockSpec((TM_LAT, MLA_HEADS * LANE), row),
            pl.BlockSpec((TM_LAT, MLA_HEADS * LANE), row),
            pl.BlockSpec((TM_LAT, w_kv2.shape[1]), row),
            pl.BlockSpec((TM_LAT, LANE), row),
        ],
        out_shape=[
            jax.ShapeDtypeStruct((m, MLA_HEADS * LANE), BF16),
            jax.ShapeDtypeStruct((m, MLA_HEADS * LANE), BF16),
            jax.ShapeDtypeStruct((m, w_kv2.shape[1]), BF16),
            jax.ShapeDtypeStruct((m, LANE), BF16),
        ],
        compiler_params=pltpu.CompilerParams(
            dimension_semantics=("parallel",), vmem_limit_bytes=VMEM_LIMIT),
        name="latent_proj",
    )(x2, g, w_lat, gq, gkv, w_q2, w_kv2, cos_t, sin_t)


def _softmax_update(s_t, v_t, m_ref, l_ref, acc_ref, col_bias=None):
    mt = jnp.max(s_t, axis=0, keepdims=True)
    if col_bias is not None:
        mt = mt + col_bias
    m_old = m_ref[...]
    m_new = jnp.maximum(m_old, mt)
    alpha = jnp.exp2(m_old - m_new)
    shift = m_new if col_bias is None else m_new - col_bias
    p = jnp.exp2(s_t - shift)
    l_ref[...] = alpha * l_ref[...] + jnp.sum(p, axis=0, keepdims=True)
    pv = lax.dot_general(v_t, p.astype(BF16), _TN, preferred_element_type=F32)
    acc_ref[...] = alpha * acc_ref[...] + pv
    m_ref[...] = m_new


def _init_stats(m_ref, l_ref, acc_ref):
    m_ref[...] = jnp.full(m_ref.shape, NEG, F32)
    l_ref[...] = jnp.zeros(l_ref.shape, F32)
    acc_ref[...] = jnp.zeros(acc_ref.shape, F32)


def _diff_attn_kernel(lam_ref, q_ref, k_ref, v_ref, gate_ref, tab_ref, cf_ref, g_ref,
                      o_ref, m_ref, l_ref, acc_ref, *, out_scale):
    tq = TQ_DIFF
    qi = pl.program_id(2)
    qb = q_ref[0]
    lane = lax.broadcasted_iota(jnp.int32, qb.shape, 1)
    zero = jnp.zeros_like(qb)
    qz = jnp.concatenate([jnp.where(lane < DIFF_HEAD_DIM, qb, zero),
                          jnp.where(lane >= DIFF_HEAD_DIM, qb, zero)], axis=0)
    _init_stats(m_ref, l_ref, acc_ref)
    cf = cf_ref[0]

    def tile(start, bias, col_bias):
        rows = pl.ds(pl.multiple_of(start, tq), tq)
        s_t = lax.dot_general(k_ref[0, rows, :], qz, _NT, preferred_element_type=F32)
        if bias is not None:
            s_t = s_t + bias
        _softmax_update(s_t, v_ref[0, rows, :], m_ref, l_ref, acc_ref, col_bias)

    def far_body(t, carry):
        tile(t * tq, None, cf)
        return carry
    lax.fori_loop(0, jnp.maximum(qi - 1, 0), far_body, 0)

    @pl.when(qi > 0)
    def _():
        tile((qi - 1) * tq, tab_ref[0, :tq, :], None)

    tile(qi * tq, tab_ref[0, tq:, :], None)

    inv = 1.0 / l_ref[...]
    lam = lam_ref[0, 0]
    o_t = acc_ref[:, :tq] * inv[:, :tq] - lam * (acc_ref[:, tq:] * inv[:, tq:])
    o_t = o_t * lax.rsqrt(jnp.mean(o_t * o_t, axis=0, keepdims=True) + EPS)
    o = o_t.T * g_ref[...] * out_scale
    o_ref[0] = (o * gate_ref[0].astype(F32)).astype(BF16)


def _diff_attn(main3, lam, tab, cf, g, lam_init):
    b, s, _ = main3.shape
    tq = TQ_DIFF
    hd = 2 * DIFF_HEAD_DIM
    kcol, vcol, gcol = DIFF_WIDTH // hd, 2 * DIFF_WIDTH // hd, QKV_WIDTH // hd
    return pl.pallas_call(
        functools.partial(_diff_attn_kernel, out_scale=1.0 - lam_init),
        grid=(b, DIFF_HEADS, s // tq),
        in_specs=[
            pl.BlockSpec(memory_space=pltpu.SMEM),
            pl.BlockSpec((1, tq, hd), lambda bi, h, qi: (bi, qi, h)),
            pl.BlockSpec((1, s, hd), lambda bi, h, qi: (bi, 0, kcol + h)),
            pl.BlockSpec((1, s, hd), lambda bi, h, qi: (bi, 0, vcol + h)),
            pl.BlockSpec((1, tq, hd), lambda bi, h, qi: (bi, qi, gcol + h)),
            pl.BlockSpec((1, 2 * tq, 2 * tq), lambda bi, h, qi: (h, 0, 0)),
            pl.BlockSpec((1, 1, 2 * tq), lambda bi, h, qi: (h, 0, 0)),
            pl.BlockSpec((1, hd), lambda bi, h, qi: (0, 0)),
        ],
        out_specs=pl.BlockSpec((1, tq, hd), lambda bi, h, qi: (bi, qi, h)),
        out_shape=jax.ShapeDtypeStruct((b, s, DIFF_WIDTH), BF16),
        scratch_shapes=[pltpu.VMEM((1, 2 * tq), F32), pltpu.VMEM((1, 2 * tq), F32),
                        pltpu.VMEM((hd, 2 * tq), F32)],
        compiler_params=pltpu.CompilerParams(
            dimension_semantics=("parallel", "parallel", "arbitrary"),
            vmem_limit_bytes=VMEM_LIMIT),
        name="diff_attn",
    )(lam, main3, main3, main3, main3, tab, cf, g)


def _mla_attn_kernel(qn_ref, qr_ref, kn_ref, kr_ref, v_ref, gate_ref, o_ref,
                     m_ref, l_ref, acc_ref):
    tq = TQ_MLA
    qi = pl.program_id(2)
    qc = jnp.concatenate([qn_ref[0], qr_ref[0]], axis=1)
    _init_stats(m_ref, l_ref, acc_ref)

    def tile(start, masked):
        rows = pl.ds(pl.multiple_of(start, tq), tq)
        kc = jnp.concatenate([kn_ref[0, rows, :], kr_ref[0, rows, :]], axis=1)
        s_t = lax.dot_general(kc, qc, _NT, preferred_element_type=F32)
        if masked:
            shift = CHUNK.bit_length() - 1
            kchunk = lax.shift_right_logical(lax.broadcasted_iota(jnp.int32, s_t.shape, 0), shift)
            qchunk = lax.shift_right_logical(lax.broadcasted_iota(jnp.int32, s_t.shape, 1), shift)
            s_t = jnp.where(kchunk <= qchunk, s_t, NEG)
        _softmax_update(s_t, v_ref[0, rows, :], m_ref, l_ref, acc_ref)

    def far_body(t, carry):
        tile(t * tq, False)
        return carry
    lax.fori_loop(0, qi, far_body, 0)
    tile(qi * tq, True)

    o_t = acc_ref[...] * (1.0 / l_ref[...])
    o_ref[0] = (o_t.T * gate_ref[0].astype(F32)).astype(BF16)


def _mla_attn(qn3, qr3, kv3, kr3, main3):
    b, s, _ = qn3.shape
    tq = TQ_MLA
    gcol = (QKV_WIDTH + DIFF_WIDTH) // LANE
    return pl.pallas_call(
        _mla_attn_kernel,
        grid=(b, MLA_HEADS, s // tq),
        in_specs=[
            pl.BlockSpec((1, tq, LANE), lambda bi, h, qi: (bi, qi, h)),
            pl.BlockSpec((1, tq, LANE), lambda bi, h, qi: (bi, qi, h)),
            pl.BlockSpec((1, s, LANE), lambda bi, h, qi: (bi, 0, 2 * h)),
            pl.BlockSpec((1, s, LANE), lambda bi, h, qi: (bi, 0, 0)),
            pl.BlockSpec((1, s, LANE), lambda bi, h, qi: (bi, 0, 2 * h + 1)),
            pl.BlockSpec((1, tq, LANE), lambda bi, h, qi: (bi, qi, gcol + h)),
        ],
        out_specs=pl.BlockSpec((1, tq, LANE), lambda bi, h, qi: (bi, qi, h)),
        out_shape=jax.ShapeDtypeStruct((b, s, MLA_WIDTH), BF16),
        scratch_shapes=[pltpu.VMEM((1, tq), F32), pltpu.VMEM((1, tq), F32),
                        pltpu.VMEM((MLA_V_DIM, tq), F32)],
        compiler_params=pltpu.CompilerParams(
            dimension_semantics=("parallel", "parallel", "arbitrary"),
            vmem_limit_bytes=VMEM_LIMIT),
        name="mla_attn",
    )(qn3, qr3, kv3, kr3, kv3, main3)


def _out_proj_kernel(x_ref, ya_ref, yb_ref, w_ref, gf_ref, o_ref, *, final):
    acc = jnp.dot(ya_ref[...], w_ref[:DIFF_WIDTH, :], preferred_element_type=F32)
    acc = acc + jnp.dot(yb_ref[...], w_ref[DIFF_WIDTH:, :], preferred_element_type=F32)
    xn = x_ref[...] + acc
    if final:
        xn = xn * _rms_scale(xn) * gf_ref[...]
    o_ref[...] = xn


def _out_proj(x2, ya, yb, w_out, gf, final):
    m = x2.shape[0]
    row = lambda i: (i, 0)
    const = lambda i: (0, 0)
    return pl.pallas_call(
        functools.partial(_out_proj_kernel, final=final),
        grid=(m // TM_OUT,),
        in_specs=[
            pl.BlockSpec((TM_OUT, D_MODEL), row),
            pl.BlockSpec((TM_OUT, DIFF_WIDTH), row),
            pl.BlockSpec((TM_OUT, MLA_WIDTH), row),
            pl.BlockSpec((MIX_WIDTH, D_MODEL), const),
            pl.BlockSpec((1, D_MODEL), const),
        ],
        out_specs=pl.BlockSpec((TM_OUT, D_MODEL), row),
        out_shape=jax.ShapeDtypeStruct((m, D_MODEL), F32),
        compiler_params=pltpu.CompilerParams(
            dimension_semantics=("parallel",), vmem_limit_bytes=VMEM_LIMIT),
        name="out_proj",
    )(x2, ya, yb, w_out, gf)


def _rel_bucket(rel):
    nb = REL_BUCKETS // 2
    max_exact = nb // 2
    ret = (rel > 0).astype(jnp.int32) * nb
    n = jnp.abs(rel)
    nf = jnp.maximum(n, 1).astype(F32)
    large = max_exact + (jnp.log(nf / max_exact) / math.log(REL_MAX_DIST / max_exact)
                         * (nb - max_exact)).astype(jnp.int32)
    large = jnp.minimum(large, nb - 1)
    return ret + jnp.where(n < max_exact, n, large)


def _bias_tables(rel_bias):
    tq = TQ_DIFF
    kk = jnp.arange(2 * tq, dtype=jnp.int32)[:, None]
    qq = jnp.arange(tq, dtype=jnp.int32)[None, :]
    bias = rel_bias[_rel_bucket(kk - tq - qq)].astype(F32) * LOG2E
    allowed = (kk < tq) | (((kk - tq) // CHUNK) <= (qq // CHUNK))
    bias = jnp.where(allowed[..., None], bias, NEG)
    tab = bias.reshape(2 * tq, tq, DIFF_HEADS, 2).transpose(2, 0, 3, 1)
    tab = tab.reshape(DIFF_HEADS, 2 * tq, 2 * tq)
    far = rel_bias[_rel_bucket(jnp.asarray(-2 * REL_MAX_DIST, jnp.int32))].astype(F32) * LOG2E
    cf = jnp.repeat(far.reshape(DIFF_HEADS, 2), tq, axis=1).reshape(DIFF_HEADS, 1, 2 * tq)
    return tab, cf


def _rope_tables(seq):
    pos = jnp.arange(seq, dtype=F32)
    inv_freq = ROPE_BASE ** (-jnp.arange(0, MLA_ROPE, 2, dtype=F32) / MLA_ROPE)
    ang = pos[:, None] * inv_freq[None, :]
    cos, sin = jnp.cos(ang), jnp.sin(ang)
    pad = jnp.zeros((seq, LANE - MLA_ROPE), F32)
    return (jnp.concatenate([cos, cos, pad], axis=1),
            jnp.concatenate([-sin, sin, pad], axis=1))


def _swap_halves(w):
    half = w.shape[-1] // 2
    return jnp.concatenate([w[..., half:], w[..., :half]], axis=-1)


def _pad_lanes(w):
    return jnp.pad(w, [(0, 0)] * (w.ndim - 1) + [(0, LANE - w.shape[-1])])


def kernel(x, norm_g, w_in, diff_lambda, diff_subln_g, mla_q_norm_g, mla_w_q_b,
           mla_kv_norm_g, mla_w_kv_b, w_out, rel_bias, final_norm_g):
    b, s, d = x.shape
    assert d == D_MODEL and s % TQ_MLA == 0 and (b * s) % TM_PROJ == 0
    m = b * s
    cos_t, sin_t = _rope_tables(s)
    tab, cf = _bias_tables(rel_bias)
    colscale = jnp.concatenate([
        jnp.full((1, DIFF_WIDTH), DIFF_HEAD_DIM ** -0.5 * LOG2E, F32),
        jnp.ones((1, MAIN_WIDTH - DIFF_WIDTH), F32)], axis=1)
    gf = final_norm_g.reshape(1, D_MODEL)

    x2 = x.reshape(m, d)
    for l in range(DEPTH):
        wl = w_in[l]
        w_main = jnp.concatenate([wl[:, :QKV_WIDTH], wl[:, GATE_OFF:]], axis=1).astype(BF16)
        w_kr = wl[:, GATE_OFF - MLA_ROPE:GATE_OFF]
        w_lat = jnp.concatenate([wl[:, LAT_OFF:GATE_OFF - MLA_ROPE], _pad_lanes(w_kr),
                                 _pad_lanes(_swap_halves(w_kr))], axis=1).astype(BF16)
        wq = mla_w_q_b[l].reshape(MLA_Q_LORA, MLA_HEADS, MLA_NOPE + MLA_ROPE)
        wq_rope = wq[:, :, MLA_NOPE:]
        w_q2 = jnp.concatenate([
            wq[:, :, :MLA_NOPE].reshape(MLA_Q_LORA, -1),
            _pad_lanes(wq_rope).reshape(MLA_Q_LORA, -1),
            _pad_lanes(_swap_halves(wq_rope)).reshape(MLA_Q_LORA, -1)], axis=1).astype(BF16)
        w_kv2 = mla_w_kv_b[l].astype(BF16)
        g = norm_g[l].reshape(1, D_MODEL)

        lam_init = 0.8 - 0.6 * math.exp(-0.3 * l)
        lp = diff_lambda[l].astype(F32)
        lam = jnp.exp(jnp.sum(lp[0] * lp[1])) - jnp.exp(jnp.sum(lp[2] * lp[3])) + lam_init

        main = _in_proj(x2, g, w_main, colscale)
        qn, qr, kv, kr = _latent(x2, g, w_lat, mla_q_norm_g[l].reshape(1, -1),
                                 mla_kv_norm_g[l].reshape(1, -1), w_q2, w_kv2, cos_t, sin_t, s)
        main3 = main.reshape(b, s, MAIN_WIDTH)
        ya = _diff_attn(main3, lam.reshape(1, 1), tab, cf,
                        diff_subln_g[l].reshape(1, -1), lam_init)
        yb = _mla_attn(qn.reshape(b, s, -1), qr.reshape(b, s, -1), kv.reshape(b, s, -1),
                       kr.reshape(b, s, -1), main3)
        x2 = _out_proj(x2, ya.reshape(m, -1), yb.reshape(m, -1), w_out[l].astype(BF16), gf,
                       final=(l == DEPTH - 1))
    return x2.reshape(b, s, d)
```

```python
import functools
import math

import jax
import jax.numpy as jnp
from jax import lax
from jax.experimental import pallas as pl
from jax.experimental.pallas import tpu as pltpu

D_MODEL = 2048
DEPTH = 2
CHUNK = 64
MIX_WIDTH = D_MODEL
DIFF_WIDTH = MIX_WIDTH // 2
MLA_WIDTH = MIX_WIDTH - DIFF_WIDTH
DIFF_HEAD_DIM = 64
DIFF_HEADS = DIFF_WIDTH // (2 * DIFF_HEAD_DIM)
MLA_V_DIM = 128
MLA_HEADS = MLA_WIDTH // MLA_V_DIM
MLA_NOPE = 128
MLA_ROPE = 64
MLA_Q_LORA = 512
MLA_KV_LORA = 256
ROPE_BASE = 10000.0
REL_BUCKETS = 32
REL_MAX_DIST = 128
EPS = 1e-6
NEG = -1e30
QKV_WIDTH = 3 * DIFF_WIDTH
LAT_OFF = QKV_WIDTH
GATE_OFF = LAT_OFF + MLA_Q_LORA + MLA_KV_LORA + MLA_ROPE
MAIN_WIDTH = QKV_WIDTH + MIX_WIDTH

LOG2E = math.log2(math.e)
LANE = 128
VMEM_LIMIT = 56 * 1024 * 1024

BF16 = jnp.bfloat16
F32 = jnp.float32

TM_PROJ = 1024
TN_PROJ = 1024
RC_NORM = 256
TM_LAT = 512
TM_OUT = 512
TQ_DIFF = 256
TQ_MLA = 512
NH_DIFF = 1
NH_MLA = 1

_TN = (((0,), (0,)), ((), ()))


def _rms_scale(xf):
    return lax.rsqrt(jnp.mean(xf * xf, axis=-1, keepdims=True) + EPS)


def _in_proj_kernel(x_ref, g_ref, w_ref, cs_ref, o_ref, h_ref, *, n_plain):
    j = pl.program_id(1)

    @pl.when(j == 0)
    def _():
        def body(c, carry):
            rows = pl.ds(pl.multiple_of(c * RC_NORM, RC_NORM), RC_NORM)
            xf = x_ref[rows, :]
            h_ref[rows, :] = (xf * _rms_scale(xf) * g_ref[...]).astype(BF16)
            return carry
        lax.fori_loop(0, TM_PROJ // RC_NORM, body, 0)

    acc = jnp.dot(h_ref[...], w_ref[...], preferred_element_type=F32) * cs_ref[...]

    @pl.when(j < n_plain)
    def _():
        o_ref[...] = acc.astype(BF16)

    @pl.when(j >= n_plain)
    def _():
        o_ref[...] = (acc * jax.nn.sigmoid(acc)).astype(BF16)


def _in_proj(x2, g, w_main, colscale):
    m = x2.shape[0]
    n = w_main.shape[1]
    return pl.pallas_call(
        functools.partial(_in_proj_kernel, n_plain=QKV_WIDTH // TN_PROJ),
        grid=(m // TM_PROJ, n // TN_PROJ),
        in_specs=[
            pl.BlockSpec((TM_PROJ, D_MODEL), lambda i, j: (i, 0)),
            pl.BlockSpec((1, D_MODEL), lambda i, j: (0, 0)),
            pl.BlockSpec((D_MODEL, TN_PROJ), lambda i, j: (0, j)),
            pl.BlockSpec((1, TN_PROJ), lambda i, j: (0, j)),
        ],
        out_specs=pl.BlockSpec((TM_PROJ, TN_PROJ), lambda i, j: (i, j)),
        out_shape=jax.ShapeDtypeStruct((m, n), BF16),
        scratch_shapes=[pltpu.VMEM((TM_PROJ, D_MODEL), BF16)],
        compiler_params=pltpu.CompilerParams(
            dimension_semantics=("parallel", "arbitrary"), vmem_limit_bytes=VMEM_LIMIT),
        name="in_proj",
    )(x2, g, w_main, colscale)


def _latent_kernel(x_ref, g_ref, wl_ref, gq_ref, gkv_ref, wq_ref, wkv_ref, cos_ref, sin_ref,
                   mq_ref, mk_ref, qn_ref, qr_ref, kv_ref, kr_ref, *, q_scale):
    xf = x_ref[...]
    h = (xf * _rms_scale(xf) * g_ref[...]).astype(BF16)
    lat = jnp.dot(h, wl_ref[...], preferred_element_type=F32)
    cq = lat[:, :MLA_Q_LORA]
    ckv = lat[:, MLA_Q_LORA:MLA_Q_LORA + MLA_KV_LORA]
    ka = lat[:, MLA_Q_LORA + MLA_KV_LORA:MLA_Q_LORA + MLA_KV_LORA + LANE]
    kb = lat[:, MLA_Q_LORA + MLA_KV_LORA + LANE:]
    cos = cos_ref[...]
    sin = sin_ref[...]
    mq = mq_ref[...]

    cqn = (cq * _rms_scale(cq) * gq_ref[...]).astype(BF16)
    qall = jnp.dot(cqn, wq_ref[...], preferred_element_type=F32) * q_scale
    nope_w = MLA_HEADS * MLA_NOPE
    qn_ref[...] = qall[:, :nope_w].astype(BF16)
    for hh in range(MLA_HEADS):
        a = qall[:, nope_w + hh * LANE:nope_w + (hh + 1) * LANE]
        b = qall[:, 2 * nope_w + hh * LANE:2 * nope_w + (hh + 1) * LANE]
        qr_ref[:, hh * LANE:(hh + 1) * LANE] = (a * cos + b * sin + mq).astype(BF16)

    ckvn = (ckv * _rms_scale(ckv) * gkv_ref[...]).astype(BF16)
    kv_ref[...] = jnp.dot(ckvn, wkv_ref[...], preferred_element_type=F32).astype(BF16)
    kr_ref[...] = (ka * cos + kb * sin + mk_ref[...]).astype(BF16)


def _latent(x2, g, w_lat, gq, gkv, w_q2, w_kv2, pos_tables, seq):
    m = x2.shape[0]
    seq_tiles = seq // TM_LAT
    const = lambda i: (0, 0)
    row = lambda i: (i, 0)
    pos = lambda i: (i % seq_tiles, 0)
    q_scale = (MLA_NOPE + MLA_ROPE) ** -0.5 * LOG2E
    return pl.pallas_call(
        functools.partial(_latent_kernel, q_scale=q_scale),
        grid=(m // TM_LAT,),
        in_specs=[
            pl.BlockSpec((TM_LAT, D_MODEL), row),
            pl.BlockSpec((1, D_MODEL), const),
            pl.BlockSpec(w_lat.shape, const),
            pl.BlockSpec((1, MLA_Q_LORA), const),
            pl.BlockSpec((1, MLA_KV_LORA), const),
            pl.BlockSpec(w_q2.shape, const),
            pl.BlockSpec(w_kv2.shape, const),
        ] + [pl.BlockSpec((TM_LAT, LANE), pos)] * len(pos_tables),
        out_specs=[
            pl.BlockSpec((TM_LAT, MLA_HEADS * LANE), row),
            pl.BlockSpec((TM_LAT, MLA_HEADS * LANE), row),
            pl.BlockSpec((TM_LAT, w_kv2.shape[1]), row),
            pl.BlockSpec((TM_LAT, LANE), row),
        ],
        out_shape=[
            jax.ShapeDtypeStruct((m, MLA_HEADS * LANE), BF16),
            jax.ShapeDtypeStruct((m, MLA_HEADS * LANE), BF16),
            jax.ShapeDtypeStruct((m, w_kv2.shape[1]), BF16),
            jax.ShapeDtypeStruct((m, LANE), BF16),
        ],
        compiler_params=pltpu.CompilerParams(
            dimension_semantics=("parallel",), vmem_limit_bytes=VMEM_LIMIT),
        name="latent_proj",
    )(x2, g, w_lat, gq, gkv, w_q2, w_kv2, *pos_tables)


def _attn_pipeline(n_q, heads, qk_operands, v_tile, bias_tile, finalize,
                   s_buf, p_buf, a_buf, lf_buf, m_ref, l_ref, acc_ref):
    last = n_q - 1
    total = n_q * (n_q + 1) // 2

    def advance(q, t):
        end = t >= q
        return (jnp.where(end, jnp.minimum(q + 1, last), q),
                jnp.where(end, jnp.where(q >= last, t, 0), t + 1))

    def qk_stage(qi, t, slot):
        for hh in heads:
            kt, qt = qk_operands(hh, qi, t)
            s_buf[hh, slot] = jnp.dot(kt, qt, preferred_element_type=F32)

    def sm_stage(qi, t, slot):
        for hh in heads:
            s = s_buf[hh, slot]
            bias = bias_tile(hh, qi, t)
            if bias is not None:
                s = s + bias
            m_old = jnp.where(t == 0, NEG, m_ref[hh])
            m_new = jnp.maximum(m_old, jnp.max(s, axis=0, keepdims=True))
            alpha = jnp.exp2(m_old - m_new)
            p = jnp.exp2(s - m_new)
            l_new = alpha * l_ref[hh] + jnp.sum(p, axis=0, keepdims=True)
            m_ref[hh] = m_new
            l_ref[hh] = l_new
            p_buf[hh, slot] = p.astype(BF16)
            a_buf[hh, slot] = alpha
            lf_buf[hh, slot] = l_new

    def pv_stage(t, slot):
        for hh in heads:
            pv = lax.dot_general(v_tile(hh, t), p_buf[hh, slot], _TN,
                                 preferred_element_type=F32)
            acc_ref[hh] = a_buf[hh, slot] * acc_ref[hh] + pv

    def finalize_all(qi, slot):
        for hh in heads:
            finalize(hh, qi, lf_buf[hh, slot], acc_ref[hh])

    m_ref[...] = jnp.full(m_ref.shape, NEG, F32)
    l_ref[...] = jnp.zeros(l_ref.shape, F32)
    acc_ref[...] = jnp.zeros(acc_ref.shape, F32)
    p_buf[...] = jnp.zeros(p_buf.shape, BF16)
    a_buf[...] = jnp.zeros(a_buf.shape, F32)
    lf_buf[...] = jnp.ones(lf_buf.shape, F32)

    zero = jnp.int32(0)
    qk_stage(zero, zero, 0)

    def step(carry, slot):
        qa, ta, qs, ts, qp, tp = carry
        other = 1 - slot
        pv_stage(tp, other)
        qk_stage(qa, ta, other)
        sm_stage(qs, ts, slot)

        @pl.when(tp == qp)
        def _():
            finalize_all(qp, other)

        na, nt = advance(qa, ta)
        return na, nt, qa, ta, qs, ts

    assert total % 2 == 0

    def body(_, carry):
        return step(step(carry, 0), 1)

    qa0, ta0 = advance(zero, zero)
    carry = lax.fori_loop(0, total // 2, body, (qa0, ta0, zero, zero, zero, jnp.int32(1)))
    end_slot = (total - 1) & 1
    pv_stage(carry[5], end_slot)
    finalize_all(carry[4], end_slot)


def _attn_scratch(nh, tk, n, dv):
    return [pltpu.VMEM((nh, 2, tk, n), F32),
            pltpu.VMEM((nh, 2, tk, n), BF16),
            pltpu.VMEM((nh, 2, 1, n), F32),
            pltpu.VMEM((nh, 2, 1, n), F32),
            pltpu.VMEM((nh, 1, n), F32),
            pltpu.VMEM((nh, 1, n), F32),
            pltpu.VMEM((nh, dv, n), F32)]


def _diff_attn_kernel(lam_ref, q_ref, k_ref, v_ref, gate_ref, tab_ref, g_ref, o_ref,
                      qz_buf, *scratch, out_scale):
    tq = TQ_DIFF
    hd = 2 * DIFF_HEAD_DIM
    n_q = q_ref.shape[1] // tq
    heads = range(NH_DIFF)
    lam = lam_ref[0, 0]

    def rows_of(i):
        return pl.ds(pl.multiple_of(i * tq, tq), tq)

    def cols_of(hh):
        return slice(hh * hd, (hh + 1) * hd)

    feat = lax.broadcasted_iota(jnp.int32, (hd, tq), 0)

    def prep(qi, carry):
        for hh in heads:
            qb = q_ref[0, rows_of(qi), cols_of(hh)].T
            zero = jnp.zeros_like(qb)
            qz_buf[hh, qi, :, :tq] = jnp.where(feat < DIFF_HEAD_DIM, qb, zero)
            qz_buf[hh, qi, :, tq:] = jnp.where(feat >= DIFF_HEAD_DIM, qb, zero)
        return carry
    lax.fori_loop(0, n_q, prep, 0)

    def qk_operands(hh, qi, t):
        return k_ref[0, rows_of(t), cols_of(hh)], qz_buf[hh, qi]

    def v_tile(hh, t):
        return v_ref[0, rows_of(t), cols_of(hh)]

    def bias_tile(hh, qi, t):
        return tab_ref[hh, jnp.maximum(t - qi + 2, 0)]

    def finalize(hh, qi, l, acc):
        inv = 1.0 / l
        o_t = acc[:, :tq] * inv[:, :tq] - lam * (acc[:, tq:] * inv[:, tq:])
        o_t = o_t * lax.rsqrt(jnp.mean(o_t * o_t, axis=0, keepdims=True) + EPS)
        o = o_t.T * g_ref[...] * out_scale
        gate = gate_ref[0, rows_of(qi), cols_of(hh)].astype(F32)
        o_ref[0, rows_of(qi), cols_of(hh)] = (o * gate).astype(BF16)

    _attn_pipeline(n_q, heads, qk_operands, v_tile, bias_tile, finalize, *scratch)


def _diff_attn(main3, lam, tab, g, lam_init):
    b, s, _ = main3.shape
    tq = TQ_DIFF
    hd = 2 * DIFF_HEAD_DIM
    nh = NH_DIFF
    blk = nh * hd
    kcol, vcol, gcol = DIFF_WIDTH // blk, 2 * DIFF_WIDTH // blk, QKV_WIDTH // blk
    seq_blk = lambda off: pl.BlockSpec((1, s, blk), lambda bi, h: (bi, 0, off + h))
    return pl.pallas_call(
        functools.partial(_diff_attn_kernel, out_scale=1.0 - lam_init),
        grid=(b, DIFF_HEADS // nh),
        in_specs=[
            pl.BlockSpec(memory_space=pltpu.SMEM),
            seq_blk(0), seq_blk(kcol), seq_blk(vcol), seq_blk(gcol),
            pl.BlockSpec((nh, 3, tq, 2 * tq), lambda bi, h: (h, 0, 0, 0)),
            pl.BlockSpec((1, hd), lambda bi, h: (0, 0)),
        ],
        out_specs=seq_blk(0),
        out_shape=jax.ShapeDtypeStruct((b, s, DIFF_WIDTH), BF16),
        scratch_shapes=[pltpu.VMEM((nh, s // tq, hd, 2 * tq), BF16)]
        + _attn_scratch(nh, tq, 2 * tq, hd),
        compiler_params=pltpu.CompilerParams(
            dimension_semantics=("parallel", "arbitrary"), vmem_limit_bytes=VMEM_LIMIT),
        name="diff_attn",
    )(lam, main3, main3, main3, main3, tab, g)


def _mla_attn_kernel(qn_ref, qr_ref, kv_ref, kr_ref, gate_ref, o_ref, qt_buf, *scratch):
    tq = TQ_MLA
    n_q = qn_ref.shape[1] // tq
    heads = range(NH_MLA)

    def rows_of(i):
        return pl.ds(pl.multiple_of(i * tq, tq), tq)

    def cols_of(hh):
        return slice(hh * LANE, (hh + 1) * LANE)

    def prep(qi, carry):
        for hh in heads:
            qt_buf[hh, qi, :LANE, :] = qn_ref[0, rows_of(qi), cols_of(hh)].T
            qt_buf[hh, qi, LANE:, :] = qr_ref[0, rows_of(qi), cols_of(hh)].T
        return carry
    lax.fori_loop(0, n_q, prep, 0)

    def qk_operands(hh, qi, t):
        kt = jnp.concatenate([kv_ref[0, rows_of(t), cols_of(2 * hh)], kr_ref[0, rows_of(t), :]],
                             axis=1)
        return kt, qt_buf[hh, qi]

    def v_tile(hh, t):
        return kv_ref[0, rows_of(t), cols_of(2 * hh + 1)]

    def finalize(hh, qi, l, acc):
        o = (acc * (1.0 / l)).T
        gate = gate_ref[0, rows_of(qi), cols_of(hh)].astype(F32)
        o_ref[0, rows_of(qi), cols_of(hh)] = (o * gate).astype(BF16)

    _attn_pipeline(n_q, heads, qk_operands, v_tile, lambda hh, qi, t: None, finalize, *scratch)


def _mla_attn(qn3, qr3, kv3, kr3, main3):
    b, s, _ = qn3.shape
    tq = TQ_MLA
    nh = NH_MLA
    blk = nh * LANE
    gcol = (QKV_WIDTH + DIFF_WIDTH) // blk
    seq_blk = lambda w, off: pl.BlockSpec((1, s, w), lambda bi, h: (bi, 0, off + h))
    return pl.pallas_call(
        _mla_attn_kernel,
        grid=(b, MLA_HEADS // nh),
        in_specs=[
            seq_blk(blk, 0), seq_blk(blk, 0), seq_blk(2 * blk, 0),
            pl.BlockSpec((1, s, LANE), lambda bi, h: (bi, 0, 0)),
            seq_blk(blk, gcol),
        ],
        out_specs=seq_blk(blk, 0),
        out_shape=jax.ShapeDtypeStruct((b, s, MLA_WIDTH), BF16),
        scratch_shapes=[pltpu.VMEM((nh, s // tq, 2 * LANE, tq), BF16)]
        + _attn_scratch(nh, tq, tq, MLA_V_DIM),
        compiler_params=pltpu.CompilerParams(
            dimension_semantics=("parallel", "arbitrary"), vmem_limit_bytes=VMEM_LIMIT),
        name="mla_attn",
    )(qn3, qr3, kv3, kr3, main3)


def _out_proj_kernel(x_ref, ya_ref, yb_ref, w_ref, gf_ref, o_ref, *, final):
    acc = jnp.dot(ya_ref[...], w_ref[:DIFF_WIDTH, :], preferred_element_type=F32)
    acc = acc + jnp.dot(yb_ref[...], w_ref[DIFF_WIDTH:, :], preferred_element_type=F32)
    xn = x_ref[...] + acc
    if final:
        xn = xn * _rms_scale(xn) * gf_ref[...]
    o_ref[...] = xn


def _out_proj(x2, ya, yb, w_out, gf, final):
    m = x2.shape[0]
    row = lambda i: (i, 0)
    const = lambda i: (0, 0)
    return pl.pallas_call(
        functools.partial(_out_proj_kernel, final=final),
        grid=(m // TM_OUT,),
        in_specs=[
            pl.BlockSpec((TM_OUT, D_MODEL), row),
            pl.BlockSpec((TM_OUT, DIFF_WIDTH), row),
            pl.BlockSpec((TM_OUT, MLA_WIDTH), row),
            pl.BlockSpec((MIX_WIDTH, D_MODEL), const),
            pl.BlockSpec((1, D_MODEL), const),
        ],
        out_specs=pl.BlockSpec((TM_OUT, D_MODEL), row),
        out_shape=jax.ShapeDtypeStruct((m, D_MODEL), F32),
        compiler_params=pltpu.CompilerParams(
            dimension_semantics=("parallel",), vmem_limit_bytes=VMEM_LIMIT),
        name="out_proj",
    )(x2, ya, yb, w_out, gf)


def _rel_bucket(rel):
    nb = REL_BUCKETS // 2
    max_exact = nb // 2
    ret = (rel > 0).astype(jnp.int32) * nb
    n = jnp.abs(rel)
    nf = jnp.maximum(n, 1).astype(F32)
    large = max_exact + (jnp.log(nf / max_exact) / math.log(REL_MAX_DIST / max_exact)
                         * (nb - max_exact)).astype(jnp.int32)
    large = jnp.minimum(large, nb - 1)
    return ret + jnp.where(n < max_exact, n, large)


def _bias_tables(rel_bias):
    tq = TQ_DIFF
    kk = jnp.arange(2 * tq, dtype=jnp.int32)[:, None]
    qq = jnp.arange(tq, dtype=jnp.int32)[None, :]
    onehot = (_rel_bucket(kk - tq - qq)[..., None] == jnp.arange(REL_BUCKETS)).astype(F32)
    bias = jnp.einsum('kqb,bc->kqc', onehot, rel_bias.astype(F32),
                      precision=lax.Precision.HIGHEST) * LOG2E
    allowed = (kk < tq) | (((kk - tq) // CHUNK) <= (qq // CHUNK))
    bias = jnp.where(allowed[..., None], bias, NEG)
    near = bias.reshape(2, tq, tq, DIFF_HEADS, 2).transpose(3, 0, 1, 4, 2)
    near = near.reshape(DIFF_HEADS, 2, tq, 2 * tq)
    far = rel_bias[_rel_bucket(jnp.asarray(-2 * REL_MAX_DIST, jnp.int32))].astype(F32) * LOG2E
    far = jnp.repeat(far.reshape(DIFF_HEADS, 2), tq, axis=1)
    far = jnp.broadcast_to(far[:, None, None, :], (DIFF_HEADS, 1, tq, 2 * tq))
    return jnp.concatenate([far, near], axis=1)


def _pos_tables(seq):
    pos = jnp.arange(seq, dtype=F32)
    inv_freq = ROPE_BASE ** (-jnp.arange(0, MLA_ROPE, 2, dtype=F32) / MLA_ROPE)
    ang = pos[:, None] * inv_freq[None, :]
    cos, sin = jnp.cos(ang), jnp.sin(ang)
    pad = jnp.zeros((seq, LANE - MLA_ROPE), F32)
    n_chunks = seq // CHUNK
    assert n_chunks <= LANE - MLA_ROPE
    chunk = (jnp.arange(seq, dtype=jnp.int32) // CHUNK)[:, None]
    cid = jnp.arange(LANE, dtype=jnp.int32)[None, :] - MLA_ROPE
    valid = (cid >= 0) & (cid < n_chunks)
    mq = jnp.where(valid & (cid > chunk), NEG, 0.0).astype(F32)
    mk = jnp.where(valid & (cid == chunk), 1.0, 0.0).astype(F32)
    return (jnp.concatenate([cos, cos, pad], axis=1),
            jnp.concatenate([-sin, sin, pad], axis=1), mq, mk)


def _swap_halves(w):
    half = w.shape[-1] // 2
    return jnp.concatenate([w[..., half:], w[..., :half]], axis=-1)


def _pad_lanes(w):
    return jnp.pad(w, [(0, 0)] * (w.ndim - 1) + [(0, LANE - w.shape[-1])])


def kernel(x, norm_g, w_in, diff_lambda, diff_subln_g, mla_q_norm_g, mla_w_q_b,
           mla_kv_norm_g, mla_w_kv_b, w_out, rel_bias, final_norm_g):
    b, s, d = x.shape
    assert d == D_MODEL and s % TQ_MLA == 0 and s >= 2 * TQ_MLA and (b * s) % TM_PROJ == 0
    m = b * s
    pos_tables = _pos_tables(s)
    tab = _bias_tables(rel_bias)
    colscale = jnp.concatenate([
        jnp.full((1, DIFF_WIDTH), DIFF_HEAD_DIM ** -0.5 * LOG2E, F32),
        jnp.ones((1, MAIN_WIDTH - DIFF_WIDTH), F32)], axis=1)
    gf = final_norm_g.reshape(1, D_MODEL)

    x2 = x.reshape(m, d)
    for l in range(DEPTH):
        wl = w_in[l]
        w_main = jnp.concatenate([wl[:, :QKV_WIDTH], wl[:, GATE_OFF:]], axis=1).astype(BF16)
        w_kr = wl[:, GATE_OFF - MLA_ROPE:GATE_OFF]
        w_lat = jnp.concatenate([wl[:, LAT_OFF:GATE_OFF - MLA_ROPE], _pad_lanes(w_kr),
                                 _pad_lanes(_swap_halves(w_kr))], axis=1).astype(BF16)
        wq = mla_w_q_b[l].reshape(MLA_Q_LORA, MLA_HEADS, MLA_NOPE + MLA_ROPE)
        wq_rope = wq[:, :, MLA_NOPE:]
        w_q2 = jnp.concatenate([
            wq[:, :, :MLA_NOPE].reshape(MLA_Q_LORA, -1),
            _pad_lanes(wq_rope).reshape(MLA_Q_LORA, -1),
            _pad_lanes(_swap_halves(wq_rope)).reshape(MLA_Q_LORA, -1)], axis=1).astype(BF16)
        w_kv2 = mla_w_kv_b[l].astype(BF16)
        g = norm_g[l].reshape(1, D_MODEL)

        lam_init = 0.8 - 0.6 * math.exp(-0.3 * l)
        lp = diff_lambda[l].astype(F32)
        lam = jnp.exp(jnp.sum(lp[0] * lp[1])) - jnp.exp(jnp.sum(lp[2] * lp[3])) + lam_init

        main = _in_proj(x2, g, w_main, colscale)
        qn, qr, kv, kr = _latent(x2, g, w_lat, mla_q_norm_g[l].reshape(1, -1),
                                 mla_kv_norm_g[l].reshape(1, -1), w_q2, w_kv2, pos_tables, s)
        main3 = main.reshape(b, s, MAIN_WIDTH)
        ya = _diff_attn(main3, lam.reshape(1, 1), tab, diff_subln_g[l].reshape(1, -1), lam_init)
        yb = _mla_attn(qn.reshape(b, s, -1), qr.reshape(b, s, -1), kv.reshape(b, s, -1),
                       kr.reshape(b, s, -1), main3)
        x2 = _out_proj(x2, ya.reshape(m, -1), yb.reshape(m, -1), w_out[l].astype(BF16), gf,
                       final=(l == DEPTH - 1))
    return x2.reshape(b, s, d)
```

```python
import functools
import math

import jax
import jax.numpy as jnp
from jax import lax
from jax.experimental import pallas as pl
from jax.experimental.pallas import tpu as pltpu

D_MODEL = 2048
DEPTH = 2
CHUNK = 64
MIX_WIDTH = D_MODEL
DIFF_WIDTH = MIX_WIDTH // 2
MLA_WIDTH = MIX_WIDTH - DIFF_WIDTH
DIFF_HEAD_DIM = 64
DIFF_HEADS = DIFF_WIDTH // (2 * DIFF_HEAD_DIM)
MLA_V_DIM = 128
MLA_HEADS = MLA_WIDTH // MLA_V_DIM
MLA_NOPE = 128
MLA_ROPE = 64
MLA_Q_LORA = 512
MLA_KV_LORA = 256
ROPE_BASE = 10000.0
REL_BUCKETS = 32
REL_MAX_DIST = 128
EPS = 1e-6
NEG = -1e30
QKV_WIDTH = 3 * DIFF_WIDTH
LAT_OFF = QKV_WIDTH
GATE_OFF = LAT_OFF + MLA_Q_LORA + MLA_KV_LORA + MLA_ROPE
MAIN_WIDTH = QKV_WIDTH + MIX_WIDTH

LOG2E = math.log2(math.e)
LANE = 128
VMEM_LIMIT = 56 * 1024 * 1024

BF16 = jnp.bfloat16
F32 = jnp.float32

TM_PROJ = 1024
TN_PROJ = 1024
RC_NORM = 256
TM_LAT = 512
TM_OUT = 512
TQ_DIFF = 256
TQ_MLA = 512
NH_DIFF = 1
NH_MLA = 1
STEPS_DIFF = 8
STEPS_MLA = 6
Q_SLOTS = 4

_TN = (((0,), (0,)), ((), ()))


def _rms_scale(xf):
    return lax.rsqrt(jnp.mean(xf * xf, axis=-1, keepdims=True) + EPS)


def _in_proj_kernel(x_ref, g_ref, w_ref, cs_ref, o_ref, h_ref, *, n_plain):
    j = pl.program_id(1)

    @pl.when(j == 0)
    def _():
        def body(c, carry):
            rows = pl.ds(pl.multiple_of(c * RC_NORM, RC_NORM), RC_NORM)
            xf = x_ref[rows, :]
            h_ref[rows, :] = (xf * _rms_scale(xf) * g_ref[...]).astype(BF16)
            return carry
        lax.fori_loop(0, TM_PROJ // RC_NORM, body, 0)

    acc = jnp.dot(h_ref[...], w_ref[...], preferred_element_type=F32) * cs_ref[...]

    @pl.when(j < n_plain)
    def _():
        o_ref[...] = acc.astype(BF16)

    @pl.when(j >= n_plain)
    def _():
        o_ref[...] = (acc * jax.nn.sigmoid(acc)).astype(BF16)


def _in_proj(x2, g, w_main, colscale):
    m = x2.shape[0]
    n = w_main.shape[1]
    return pl.pallas_call(
        functools.partial(_in_proj_kernel, n_plain=QKV_WIDTH // TN_PROJ),
        grid=(m // TM_PROJ, n // TN_PROJ),
        in_specs=[
            pl.BlockSpec((TM_PROJ, D_MODEL), lambda i, j: (i, 0)),
            pl.BlockSpec((1, D_MODEL), lambda i, j: (0, 0)),
            pl.BlockSpec((D_MODEL, TN_PROJ), lambda i, j: (0, j)),
            pl.BlockSpec((1, TN_PROJ), lambda i, j: (0, j)),
        ],
        out_specs=pl.BlockSpec((TM_PROJ, TN_PROJ), lambda i, j: (i, j)),
        out_shape=jax.ShapeDtypeStruct((m, n), BF16),
        scratch_shapes=[pltpu.VMEM((TM_PROJ, D_MODEL), BF16)],
        compiler_params=pltpu.CompilerParams(
            dimension_semantics=("parallel", "arbitrary"), vmem_limit_bytes=VMEM_LIMIT),
        name="in_proj",
    )(x2, g, w_main, colscale)


def _latent_kernel(x_ref, g_ref, wl_ref, gq_ref, gkv_ref, wq_ref, wkv_ref, cos_ref, sin_ref,
                   mq_ref, mk_ref, qn_ref, qr_ref, kv_ref, kr_ref, *, q_scale):
    xf = x_ref[...]
    h = (xf * _rms_scale(xf) * g_ref[...]).astype(BF16)
    lat = jnp.dot(h, wl_ref[...], preferred_element_type=F32)
    cq = lat[:, :MLA_Q_LORA]
    ckv = lat[:, MLA_Q_LORA:MLA_Q_LORA + MLA_KV_LORA]
    ka = lat[:, MLA_Q_LORA + MLA_KV_LORA:MLA_Q_LORA + MLA_KV_LORA + LANE]
    kb = lat[:, MLA_Q_LORA + MLA_KV_LORA + LANE:]
    cos = cos_ref[...]
    sin = sin_ref[...]
    mq = mq_ref[...]

    cqn = (cq * _rms_scale(cq) * gq_ref[...]).astype(BF16)
    qall = jnp.dot(cqn, wq_ref[...], preferred_element_type=F32) * q_scale
    nope_w = MLA_HEADS * MLA_NOPE
    qn_ref[...] = qall[:, :nope_w].astype(BF16)
    for hh in range(MLA_HEADS):
        a = qall[:, nope_w + hh * LANE:nope_w + (hh + 1) * LANE]
        b = qall[:, 2 * nope_w + hh * LANE:2 * nope_w + (hh + 1) * LANE]
        qr_ref[:, hh * LANE:(hh + 1) * LANE] = (a * cos + b * sin + mq).astype(BF16)

    ckvn = (ckv * _rms_scale(ckv) * gkv_ref[...]).astype(BF16)
    kv_ref[...] = jnp.dot(ckvn, wkv_ref[...], preferred_element_type=F32).astype(BF16)
    kr_ref[...] = (ka * cos + kb * sin + mk_ref[...]).astype(BF16)


def _latent(x2, g, w_lat, gq, gkv, w_q2, w_kv2, pos_tables, seq):
    m = x2.shape[0]
    seq_tiles = seq // TM_LAT
    const = lambda i: (0, 0)
    row = lambda i: (i, 0)
    pos = lambda i: (i % seq_tiles, 0)
    q_scale = (MLA_NOPE + MLA_ROPE) ** -0.5 * LOG2E
    return pl.pallas_call(
        functools.partial(_latent_kernel, q_scale=q_scale),
        grid=(m // TM_LAT,),
        in_specs=[
            pl.BlockSpec((TM_LAT, D_MODEL), row),
            pl.BlockSpec((1, D_MODEL), const),
            pl.BlockSpec(w_lat.shape, const),
            pl.BlockSpec((1, MLA_Q_LORA), const),
            pl.BlockSpec((1, MLA_KV_LORA), const),
            pl.BlockSpec(w_q2.shape, const),
            pl.BlockSpec(w_kv2.shape, const),
        ] + [pl.BlockSpec((TM_LAT, LANE), pos)] * len(pos_tables),
        out_specs=[
            pl.BlockSpec((TM_LAT, MLA_HEADS * LANE), row),
            pl.BlockSpec((TM_LAT, MLA_HEADS * LANE), row),
            pl.BlockSpec((TM_LAT, w_kv2.shape[1]), row),
            pl.BlockSpec((TM_LAT, LANE), row),
        ],
        out_shape=[
            jax.ShapeDtypeStruct((m, MLA_HEADS * LANE), BF16),
            jax.ShapeDtypeStruct((m, MLA_HEADS * LANE), BF16),
            jax.ShapeDtypeStruct((m, w_kv2.shape[1]), BF16),
            jax.ShapeDtypeStruct((m, LANE), BF16),
        ],
        compiler_params=pltpu.CompilerParams(
            dimension_semantics=("parallel",), vmem_limit_bytes=VMEM_LIMIT),
        name="latent_proj",
    )(x2, g, w_lat, gq, gkv, w_q2, w_kv2, *pos_tables)


def _attn_pipeline(n_q, steps, heads, qk_operands, v_tile, bias_tile, finalize,
                   s_buf, p_buf, a_buf, lf_buf, m_ref, l_ref, acc_ref):
    last = n_q - 1
    total = n_q * (n_q + 1) // 2

    def advance(q, t):
        end = t >= q
        return (jnp.where(end, jnp.minimum(q + 1, last), q),
                jnp.where(end, jnp.where(q >= last, t, 0), t + 1))

    def qk_stage(qi, t, slot):
        for hh in heads:
            kt, qt = qk_operands(hh, qi, t)
            s_buf[hh, slot] = jnp.dot(kt, qt, preferred_element_type=F32)

    def sm_stage(qi, t, slot):
        for hh in heads:
            s = s_buf[hh, slot]
            bias = bias_tile(hh, qi, t)
            if bias is not None:
                s = s + bias
            m_old = jnp.where(t == 0, NEG, m_ref[hh])
            m_new = jnp.maximum(m_old, jnp.max(s, axis=0, keepdims=True))
            alpha = jnp.exp2(m_old - m_new)
            p = jnp.exp2(s - m_new)
            l_new = alpha * l_ref[hh] + jnp.sum(p, axis=0, keepdims=True)
            m_ref[hh] = m_new
            l_ref[hh] = l_new
            p_buf[hh, slot] = p.astype(BF16)
            a_buf[hh, slot] = alpha
            lf_buf[hh, qi & (Q_SLOTS - 1)] = l_new

    def pv_stage(qi, t, slot):
        for hh in heads:
            pv = lax.dot_general(v_tile(hh, t), p_buf[hh, slot], _TN,
                                 preferred_element_type=F32)
            qs = qi & (Q_SLOTS - 1)
            acc_ref[hh, qs] = a_buf[hh, slot] * acc_ref[hh, qs] + pv

    def finalize_all(qi):
        for hh in heads:
            qs = qi & (Q_SLOTS - 1)
            finalize(hh, qi, lf_buf[hh, qs], acc_ref[hh, qs])

    m_ref[...] = jnp.full(m_ref.shape, NEG, F32)
    l_ref[...] = jnp.zeros(l_ref.shape, F32)
    acc_ref[...] = jnp.zeros(acc_ref.shape, F32)
    p_buf[...] = jnp.zeros(p_buf.shape, BF16)
    a_buf[...] = jnp.zeros(a_buf.shape, F32)

    zero = jnp.int32(0)
    qk_stage(zero, zero, 0)

    def step(carry, slot):
        qa, ta, qs, ts, qp, tp = carry
        other = 1 - slot
        pv_stage(qp, tp, other)
        qk_stage(qa, ta, other)
        sm_stage(qs, ts, slot)
        na, nt = advance(qa, ta)
        return (na, nt, qa, ta, qs, ts), (tp == qp, qp)

    assert total % steps == 0 and steps % 2 == 0 and steps <= 8 and Q_SLOTS == 4

    def body(_, carry):
        done = []
        for j in range(steps):
            carry, d = step(carry, j % 2)
            done.append(d)
        for ended, q in done:
            @pl.when(ended)
            def _():
                finalize_all(q)
        return carry

    qa0, ta0 = advance(zero, zero)
    carry = lax.fori_loop(0, total // steps, body,
                          (qa0, ta0, zero, zero, zero, jnp.int32(1)))
    pv_stage(carry[4], carry[5], (total - 1) % 2)
    finalize_all(carry[4])


def _attn_scratch(nh, tk, n, dv):
    return [pltpu.VMEM((nh, 2, tk, n), F32),
            pltpu.VMEM((nh, 2, tk, n), BF16),
            pltpu.VMEM((nh, 2, 1, n), F32),
            pltpu.VMEM((nh, Q_SLOTS, 1, n), F32),
            pltpu.VMEM((nh, 1, n), F32),
            pltpu.VMEM((nh, 1, n), F32),
            pltpu.VMEM((nh, Q_SLOTS, dv, n), F32)]


def _diff_attn_kernel(lam_ref, q_ref, k_ref, v_ref, gate_ref, tab_ref, g_ref, o_ref,
                      qz_buf, *scratch, out_scale):
    tq = TQ_DIFF
    hd = 2 * DIFF_HEAD_DIM
    n_q = q_ref.shape[1] // tq
    heads = range(NH_DIFF)
    lam = lam_ref[0, 0]

    def rows_of(i):
        return pl.ds(pl.multiple_of(i * tq, tq), tq)

    def cols_of(hh):
        return slice(hh * hd, (hh + 1) * hd)

    feat = lax.broadcasted_iota(jnp.int32, (hd, tq), 0)

    def prep(qi, carry):
        for hh in heads:
            qb = q_ref[0, rows_of(qi), cols_of(hh)].T
            zero = jnp.zeros_like(qb)
            qz_buf[hh, qi, :, :tq] = jnp.where(feat < DIFF_HEAD_DIM, qb, zero)
            qz_buf[hh, qi, :, tq:] = jnp.where(feat >= DIFF_HEAD_DIM, qb, zero)
        return carry
    lax.fori_loop(0, n_q, prep, 0)

    def qk_operands(hh, qi, t):
        return k_ref[0, rows_of(t), cols_of(hh)], qz_buf[hh, qi]

    def v_tile(hh, t):
        return v_ref[0, rows_of(t), cols_of(hh)]

    def bias_tile(hh, qi, t):
        return tab_ref[hh, jnp.maximum(t - qi + 2, 0)]

    def finalize(hh, qi, l, acc):
        inv = 1.0 / l
        o_t = acc[:, :tq] * inv[:, :tq] - lam * (acc[:, tq:] * inv[:, tq:])
        o_t = o_t * lax.rsqrt(jnp.mean(o_t * o_t, axis=0, keepdims=True) + EPS)
        o = o_t.T * g_ref[...] * out_scale
        gate = gate_ref[0, rows_of(qi), cols_of(hh)].astype(F32)
        o_ref[0, rows_of(qi), cols_of(hh)] = (o * gate).astype(BF16)

    _attn_pipeline(n_q, STEPS_DIFF, heads, qk_operands, v_tile, bias_tile, finalize, *scratch)


def _diff_attn(main3, lam, tab, g, lam_init):
    b, s, _ = main3.shape
    tq = TQ_DIFF
    hd = 2 * DIFF_HEAD_DIM
    nh = NH_DIFF
    blk = nh * hd
    kcol, vcol, gcol = DIFF_WIDTH // blk, 2 * DIFF_WIDTH // blk, QKV_WIDTH // blk
    seq_blk = lambda off: pl.BlockSpec((1, s, blk), lambda bi, h: (bi, 0, off + h))
    return pl.pallas_call(
        functools.partial(_diff_attn_kernel, out_scale=1.0 - lam_init),
        grid=(b, DIFF_HEADS // nh),
        in_specs=[
            pl.BlockSpec(memory_space=pltpu.SMEM),
            seq_blk(0), seq_blk(kcol), seq_blk(vcol), seq_blk(gcol),
            pl.BlockSpec((nh, 3, tq, 2 * tq), lambda bi, h: (h, 0, 0, 0)),
            pl.BlockSpec((1, hd), lambda bi, h: (0, 0)),
        ],
        out_specs=seq_blk(0),
        out_shape=jax.ShapeDtypeStruct((b, s, DIFF_WIDTH), BF16),
        scratch_shapes=[pltpu.VMEM((nh, s // tq, hd, 2 * tq), BF16)]
        + _attn_scratch(nh, tq, 2 * tq, hd),
        compiler_params=pltpu.CompilerParams(
            dimension_semantics=("parallel", "arbitrary"), vmem_limit_bytes=VMEM_LIMIT),
        name="diff_attn",
    )(lam, main3, main3, main3, main3, tab, g)


def _mla_attn_kernel(qn_ref, qr_ref, kv_ref, kr_ref, gate_ref, o_ref, qt_buf, *scratch):
    tq = TQ_MLA
    n_q = qn_ref.shape[1] // tq
    heads = range(NH_MLA)

    def rows_of(i):
        return pl.ds(pl.multiple_of(i * tq, tq), tq)

    def cols_of(hh):
        return slice(hh * LANE, (hh + 1) * LANE)

    def prep(qi, carry):
        for hh in heads:
            qt_buf[hh, qi, :LANE, :] = qn_ref[0, rows_of(qi), cols_of(hh)].T
            qt_buf[hh, qi, LANE:, :] = qr_ref[0, rows_of(qi), cols_of(hh)].T
        return carry
    lax.fori_loop(0, n_q, prep, 0)

    def qk_operands(hh, qi, t):
        kt = jnp.concatenate([kv_ref[0, rows_of(t), cols_of(2 * hh)], kr_ref[0, rows_of(t), :]],
                             axis=1)
        return kt, qt_buf[hh, qi]

    def v_tile(hh, t):
        return kv_ref[0, rows_of(t), cols_of(2 * hh + 1)]

    def finalize(hh, qi, l, acc):
        o = (acc * (1.0 / l)).T
        gate = gate_ref[0, rows_of(qi), cols_of(hh)].astype(F32)
        o_ref[0, rows_of(qi), cols_of(hh)] = (o * gate).astype(BF16)

    _attn_pipeline(n_q, STEPS_MLA, heads, qk_operands, v_tile, lambda hh, qi, t: None, finalize,
                   *scratch)


def _mla_attn(qn3, qr3, kv3, kr3, main3):
    b, s, _ = qn3.shape
    tq = TQ_MLA
    nh = NH_MLA
    blk = nh * LANE
    gcol = (QKV_WIDTH + DIFF_WIDTH) // blk
    seq_blk = lambda w, off: pl.BlockSpec((1, s, w), lambda bi, h: (bi, 0, off + h))
    return pl.pallas_call(
        _mla_attn_kernel,
        grid=(b, MLA_HEADS // nh),
        in_specs=[
            seq_blk(blk, 0), seq_blk(blk, 0), seq_blk(2 * blk, 0),
            pl.BlockSpec((1, s, LANE), lambda bi, h: (bi, 0, 0)),
            seq_blk(blk, gcol),
        ],
        out_specs=seq_blk(blk, 0),
        out_shape=jax.ShapeDtypeStruct((b, s, MLA_WIDTH), BF16),
        scratch_shapes=[pltpu.VMEM((nh, s // tq, 2 * LANE, tq), BF16)]
        + _attn_scratch(nh, tq, tq, MLA_V_DIM),
        compiler_params=pltpu.CompilerParams(
            dimension_semantics=("parallel", "arbitrary"), vmem_limit_bytes=VMEM_LIMIT),
        name="mla_attn",
    )(qn3, qr3, kv3, kr3, main3)


def _out_proj_kernel(x_ref, ya_ref, yb_ref, w_ref, gf_ref, o_ref, *, final):
    acc = jnp.dot(ya_ref[...], w_ref[:DIFF_WIDTH, :], preferred_element_type=F32)
    acc = acc + jnp.dot(yb_ref[...], w_ref[DIFF_WIDTH:, :], preferred_element_type=F32)
    xn = x_ref[...] + acc
    if final:
        xn = xn * _rms_scale(xn) * gf_ref[...]
    o_ref[...] = xn


def _out_proj(x2, ya, yb, w_out, gf, final):
    m = x2.shape[0]
    row = lambda i: (i, 0)
    const = lambda i: (0, 0)
    return pl.pallas_call(
        functools.partial(_out_proj_kernel, final=final),
        grid=(m // TM_OUT,),
        in_specs=[
            pl.BlockSpec((TM_OUT, D_MODEL), row),
            pl.BlockSpec((TM_OUT, DIFF_WIDTH), row),
            pl.BlockSpec((TM_OUT, MLA_WIDTH), row),
            pl.BlockSpec((MIX_WIDTH, D_MODEL), const),
            pl.BlockSpec((1, D_MODEL), const),
        ],
        out_specs=pl.BlockSpec((TM_OUT, D_MODEL), row),
        out_shape=jax.ShapeDtypeStruct((m, D_MODEL), F32),
        compiler_params=pltpu.CompilerParams(
            dimension_semantics=("parallel",), vmem_limit_bytes=VMEM_LIMIT),
        name="out_proj",
    )(x2, ya, yb, w_out, gf)


def _rel_bucket(rel):
    nb = REL_BUCKETS // 2
    max_exact = nb // 2
    ret = (rel > 0).astype(jnp.int32) * nb
    n = jnp.abs(rel)
    nf = jnp.maximum(n, 1).astype(F32)
    large = max_exact + (jnp.log(nf / max_exact) / math.log(REL_MAX_DIST / max_exact)
                         * (nb - max_exact)).astype(jnp.int32)
    large = jnp.minimum(large, nb - 1)
    return ret + jnp.where(n < max_exact, n, large)


def _bias_tables(rel_bias):
    tq = TQ_DIFF
    kk = jnp.arange(2 * tq, dtype=jnp.int32)[:, None]
    qq = jnp.arange(tq, dtype=jnp.int32)[None, :]
    onehot = (_rel_bucket(kk - tq - qq)[..., None] == jnp.arange(REL_BUCKETS)).astype(F32)
    bias = jnp.einsum('kqb,bc->kqc', onehot, rel_bias.astype(F32),
                      precision=lax.Precision.HIGHEST) * LOG2E
    allowed = (kk < tq) | (((kk - tq) // CHUNK) <= (qq // CHUNK))
    bias = jnp.where(allowed[..., None], bias, NEG)
    near = bias.reshape(2, tq, tq, DIFF_HEADS, 2).transpose(3, 0, 1, 4, 2)
    near = near.reshape(DIFF_HEADS, 2, tq, 2 * tq)
    far = rel_bias[_rel_bucket(jnp.asarray(-2 * REL_MAX_DIST, jnp.int32))].astype(F32) * LOG2E
    far = jnp.repeat(far.reshape(DIFF_HEADS, 2), tq, axis=1)
    far = jnp.broadcast_to(far[:, None, None, :], (DIFF_HEADS, 1, tq, 2 * tq))
    return jnp.concatenate([far, near], axis=1)


def _pos_tables(seq):
    pos = jnp.arange(seq, dtype=F32)
    inv_freq = ROPE_BASE ** (-jnp.arange(0, MLA_ROPE, 2, dtype=F32) / MLA_ROPE)
    ang = pos[:, None] * inv_freq[None, :]
    cos, sin = jnp.cos(ang), jnp.sin(ang)
    pad = jnp.zeros((seq, LANE - MLA_ROPE), F32)
    n_chunks = seq // CHUNK
    assert n_chunks <= LANE - MLA_ROPE
    chunk = (jnp.arange(seq, dtype=jnp.int32) // CHUNK)[:, None]
    cid = jnp.arange(LANE, dtype=jnp.int32)[None, :] - MLA_ROPE
    valid = (cid >= 0) & (cid < n_chunks)
    mq = jnp.where(valid & (cid > chunk), NEG, 0.0).astype(F32)
    mk = jnp.where(valid & (cid == chunk), 1.0, 0.0).astype(F32)
    return (jnp.concatenate([cos, cos, pad], axis=1),
            jnp.concatenate([-sin, sin, pad], axis=1), mq, mk)


def _swap_halves(w):
    half = w.shape[-1] // 2
    return jnp.concatenate([w[..., half:], w[..., :half]], axis=-1)


def _pad_lanes(w):
    return jnp.pad(w, [(0, 0)] * (w.ndim - 1) + [(0, LANE - w.shape[-1])])


def kernel(x, norm_g, w_in, diff_lambda, diff_subln_g, mla_q_norm_g, mla_w_q_b,
           mla_kv_norm_g, mla_w_kv_b, w_out, rel_bias, final_norm_g):
    b, s, d = x.shape
    assert d == D_MODEL and s % TQ_MLA == 0 and s >= 2 * TQ_MLA and (b * s) % TM_PROJ == 0
    m = b * s
    pos_tables = _pos_tables(s)
    tab = _bias_tables(rel_bias)
    colscale = jnp.concatenate([
        jnp.full((1, DIFF_WIDTH), DIFF_HEAD_DIM ** -0.5 * LOG2E, F32),
        jnp.ones((1, MAIN_WIDTH - DIFF_WIDTH), F32)], axis=1)
    gf = final_norm_g.reshape(1, D_MODEL)

    x2 = x.reshape(m, d)
    for l in range(DEPTH):
        wl = w_in[l]
        w_main = jnp.concatenate([wl[:, :QKV_WIDTH], wl[:, GATE_OFF:]], axis=1).astype(BF16)
        w_kr = wl[:, GATE_OFF - MLA_ROPE:GATE_OFF]
        w_lat = jnp.concatenate([wl[:, LAT_OFF:GATE_OFF - MLA_ROPE], _pad_lanes(w_kr),
                                 _pad_lanes(_swap_halves(w_kr))], axis=1).astype(BF16)
        wq = mla_w_q_b[l].reshape(MLA_Q_LORA, MLA_HEADS, MLA_NOPE + MLA_ROPE)
        wq_rope = wq[:, :, MLA_NOPE:]
        w_q2 = jnp.concatenate([
            wq[:, :, :MLA_NOPE].reshape(MLA_Q_LORA, -1),
            _pad_lanes(wq_rope).reshape(MLA_Q_LORA, -1),
            _pad_lanes(_swap_halves(wq_rope)).reshape(MLA_Q_LORA, -1)], axis=1).astype(BF16)
        w_kv2 = mla_w_kv_b[l].astype(BF16)
        g = norm_g[l].reshape(1, D_MODEL)

        lam_init = 0.8 - 0.6 * math.exp(-0.3 * l)
        lp = diff_lambda[l].astype(F32)
        lam = jnp.exp(jnp.sum(lp[0] * lp[1])) - jnp.exp(jnp.sum(lp[2] * lp[3])) + lam_init

        main = _in_proj(x2, g, w_main, colscale)
        qn, qr, kv, kr = _latent(x2, g, w_lat, mla_q_norm_g[l].reshape(1, -1),
                                 mla_kv_norm_g[l].reshape(1, -1), w_q2, w_kv2, pos_tables, s)
        main3 = main.reshape(b, s, MAIN_WIDTH)
        ya = _diff_attn(main3, lam.reshape(1, 1), tab, diff_subln_g[l].reshape(1, -1), lam_init)
        yb = _mla_attn(qn.reshape(b, s, -1), qr.reshape(b, s, -1), kv.reshape(b, s, -1),
                       kr.reshape(b, s, -1), main3)
        x2 = _out_proj(x2, ya.reshape(m, -1), yb.reshape(m, -1), w_out[l].astype(BF16), gf,
                       final=(l == DEPTH - 1))
    return x2.reshape(b, s, d)
```

```python
import functools
import math

import jax
import jax.numpy as jnp
from jax import lax
from jax.experimental import pallas as pl
from jax.experimental.pallas import tpu as pltpu

D_MODEL = 2048
DEPTH = 2
CHUNK = 64
MIX_WIDTH = D_MODEL
DIFF_WIDTH = MIX_WIDTH // 2
MLA_WIDTH = MIX_WIDTH - DIFF_WIDTH
DIFF_HEAD_DIM = 64
DIFF_HEADS = DIFF_WIDTH // (2 * DIFF_HEAD_DIM)
MLA_V_DIM = 128
MLA_HEADS = MLA_WIDTH // MLA_V_DIM
MLA_NOPE = 128
MLA_ROPE = 64
MLA_Q_LORA = 512
MLA_KV_LORA = 256
ROPE_BASE = 10000.0
REL_BUCKETS = 32
REL_MAX_DIST = 128
EPS = 1e-6
NEG = -1e30
QKV_WIDTH = 3 * DIFF_WIDTH
LAT_OFF = QKV_WIDTH
GATE_OFF = LAT_OFF + MLA_Q_LORA + MLA_KV_LORA + MLA_ROPE
MAIN_WIDTH = QKV_WIDTH + MIX_WIDTH

LOG2E = math.log2(math.e)
LANE = 128
VMEM_LIMIT = 56 * 1024 * 1024

BF16 = jnp.bfloat16
F32 = jnp.float32

TM_PROJ = 1024
TN_PROJ = 1024
RC_NORM = 256
TM_LAT = 512
TM_OUT = 512
TQ_DIFF = 256
TQ_MLA = 512
NH_DIFF = 1
NH_MLA = 1
STEPS_DIFF = 8
STEPS_MLA = 6
Q_SLOTS = 4

DV_AUG = MLA_V_DIM + 16


def _rms_scale(xf):
    return lax.rsqrt(jnp.mean(xf * xf, axis=-1, keepdims=True) + EPS)


def _in_proj_kernel(x_ref, g_ref, w_ref, cs_ref, o_ref, h_ref, *, n_plain):
    j = pl.program_id(1)

    @pl.when(j == 0)
    def _():
        def body(c, carry):
            rows = pl.ds(pl.multiple_of(c * RC_NORM, RC_NORM), RC_NORM)
            xf = x_ref[rows, :]
            h_ref[rows, :] = (xf * _rms_scale(xf) * g_ref[...]).astype(BF16)
            return carry
        lax.fori_loop(0, TM_PROJ // RC_NORM, body, 0)

    acc = jnp.dot(h_ref[...], w_ref[...], preferred_element_type=F32) * cs_ref[...]

    @pl.when(j < n_plain)
    def _():
        o_ref[...] = acc.astype(BF16)

    @pl.when(j >= n_plain)
    def _():
        o_ref[...] = (acc * jax.nn.sigmoid(acc)).astype(BF16)


def _in_proj(x2, g, w_main, colscale):
    m = x2.shape[0]
    n = w_main.shape[1]
    return pl.pallas_call(
        functools.partial(_in_proj_kernel, n_plain=QKV_WIDTH // TN_PROJ),
        grid=(m // TM_PROJ, n // TN_PROJ),
        in_specs=[
            pl.BlockSpec((TM_PROJ, D_MODEL), lambda i, j: (i, 0)),
            pl.BlockSpec((1, D_MODEL), lambda i, j: (0, 0)),
            pl.BlockSpec((D_MODEL, TN_PROJ), lambda i, j: (0, j)),
            pl.BlockSpec((1, TN_PROJ), lambda i, j: (0, j)),
        ],
        out_specs=pl.BlockSpec((TM_PROJ, TN_PROJ), lambda i, j: (i, j)),
        out_shape=jax.ShapeDtypeStruct((m, n), BF16),
        scratch_shapes=[pltpu.VMEM((TM_PROJ, D_MODEL), BF16)],
        compiler_params=pltpu.CompilerParams(
            dimension_semantics=("parallel", "arbitrary"), vmem_limit_bytes=VMEM_LIMIT),
        name="in_proj",
    )(x2, g, w_main, colscale)


def _latent_kernel(x_ref, g_ref, wl_ref, gq_ref, gkv_ref, wq_ref, wkv_ref, cos_ref, sin_ref,
                   mq_ref, mk_ref, qn_ref, qr_ref, kv_ref, kr_ref, *, q_scale):
    xf = x_ref[...]
    h = (xf * _rms_scale(xf) * g_ref[...]).astype(BF16)
    lat = jnp.dot(h, wl_ref[...], preferred_element_type=F32)
    cq = lat[:, :MLA_Q_LORA]
    ckv = lat[:, MLA_Q_LORA:MLA_Q_LORA + MLA_KV_LORA]
    ka = lat[:, MLA_Q_LORA + MLA_KV_LORA:MLA_Q_LORA + MLA_KV_LORA + LANE]
    kb = lat[:, MLA_Q_LORA + MLA_KV_LORA + LANE:]
    cos = cos_ref[...]
    sin = sin_ref[...]
    mq = mq_ref[...]

    cqn = (cq * _rms_scale(cq) * gq_ref[...]).astype(BF16)
    qall = jnp.dot(cqn, wq_ref[...], preferred_element_type=F32) * q_scale
    nope_w = MLA_HEADS * MLA_NOPE
    qn_ref[...] = qall[:, :nope_w].astype(BF16)
    for hh in range(MLA_HEADS):
        a = qall[:, nope_w + hh * LANE:nope_w + (hh + 1) * LANE]
        b = qall[:, 2 * nope_w + hh * LANE:2 * nope_w + (hh + 1) * LANE]
        qr_ref[:, hh * LANE:(hh + 1) * LANE] = (a * cos + b * sin + mq).astype(BF16)

    ckvn = (ckv * _rms_scale(ckv) * gkv_ref[...]).astype(BF16)
    kv_ref[...] = jnp.dot(ckvn, wkv_ref[...], preferred_element_type=F32).astype(BF16)
    kr_ref[...] = (ka * cos + kb * sin + mk_ref[...]).astype(BF16)


def _latent(x2, g, w_lat, gq, gkv, w_q2, w_kv2, pos_tables, seq):
    m = x2.shape[0]
    seq_tiles = seq // TM_LAT
    const = lambda i: (0, 0)
    row = lambda i: (i, 0)
    pos = lambda i: (i % seq_tiles, 0)
    q_scale = (MLA_NOPE + MLA_ROPE) ** -0.5 * LOG2E
    return pl.pallas_call(
        functools.partial(_latent_kernel, q_scale=q_scale),
        grid=(m // TM_LAT,),
        in_specs=[
            pl.BlockSpec((TM_LAT, D_MODEL), row),
            pl.BlockSpec((1, D_MODEL), const),
            pl.BlockSpec(w_lat.shape, const),
            pl.BlockSpec((1, MLA_Q_LORA), const),
            pl.BlockSpec((1, MLA_KV_LORA), const),
            pl.BlockSpec(w_q2.shape, const),
            pl.BlockSpec(w_kv2.shape, const),
        ] + [pl.BlockSpec((TM_LAT, LANE), pos)] * len(pos_tables),
        out_specs=[
            pl.BlockSpec((TM_LAT, MLA_HEADS * LANE), row),
            pl.BlockSpec((TM_LAT, MLA_HEADS * LANE), row),
            pl.BlockSpec((TM_LAT, w_kv2.shape[1]), row),
            pl.BlockSpec((TM_LAT, LANE), row),
        ],
        out_shape=[
            jax.ShapeDtypeStruct((m, MLA_HEADS * LANE), BF16),
            jax.ShapeDtypeStruct((m, MLA_HEADS * LANE), BF16),
            jax.ShapeDtypeStruct((m, w_kv2.shape[1]), BF16),
            jax.ShapeDtypeStruct((m, LANE), BF16),
        ],
        compiler_params=pltpu.CompilerParams(
            dimension_semantics=("parallel",), vmem_limit_bytes=VMEM_LIMIT),
        name="latent_proj",
    )(x2, g, w_lat, gq, gkv, w_q2, w_kv2, *pos_tables)


def _attn_pipeline(n_q, steps, heads, qk_operands, vt_tile, bias_tile, finalize,
                   s_buf, p_buf, a_buf, m_ref, acc_ref):
    last = n_q - 1
    total = n_q * (n_q + 1) // 2

    def advance(q, t):
        end = t >= q
        return (jnp.where(end, jnp.minimum(q + 1, last), q),
                jnp.where(end, jnp.where(q >= last, t, 0), t + 1))

    def qk_stage(qi, t, slot):
        for hh in heads:
            kt, qt = qk_operands(hh, qi, t)
            s_buf[hh, slot] = jnp.dot(kt, qt, preferred_element_type=F32)

    def sm_stage(qi, t, slot):
        for hh in heads:
            s = s_buf[hh, slot]
            bias = bias_tile(hh, qi, t)
            if bias is not None:
                s = s + bias
            m_old = jnp.where(t == 0, NEG, m_ref[hh])
            m_new = jnp.maximum(m_old, jnp.max(s, axis=0, keepdims=True))
            m_ref[hh] = m_new
            p_buf[hh, slot] = jnp.exp2(s - m_new).astype(BF16)
            a_buf[hh, slot] = jnp.exp2(m_old - m_new)

    def pv_stage(qi, t, slot):
        for hh in heads:
            pv = jnp.dot(vt_tile(hh, t), p_buf[hh, slot], preferred_element_type=F32)
            qs = qi & (Q_SLOTS - 1)
            acc_ref[hh, qs] = a_buf[hh, slot] * acc_ref[hh, qs] + pv

    def finalize_all(qi):
        for hh in heads:
            finalize(hh, qi, acc_ref[hh, qi & (Q_SLOTS - 1)])

    m_ref[...] = jnp.full(m_ref.shape, NEG, F32)
    acc_ref[...] = jnp.zeros(acc_ref.shape, F32)
    p_buf[...] = jnp.zeros(p_buf.shape, BF16)
    a_buf[...] = jnp.zeros(a_buf.shape, F32)

    zero = jnp.int32(0)
    qk_stage(zero, zero, 0)

    def step(carry, slot):
        qa, ta, qs, ts, qp, tp = carry
        other = 1 - slot
        pv_stage(qp, tp, other)
        qk_stage(qa, ta, other)
        sm_stage(qs, ts, slot)
        na, nt = advance(qa, ta)
        return (na, nt, qa, ta, qs, ts), (tp == qp, qp)

    assert total % steps == 0 and steps % 2 == 0 and steps <= 8 and Q_SLOTS == 4

    def body(_, carry):
        done = []
        for j in range(steps):
            carry, d = step(carry, j % 2)
            done.append(d)
        for ended, q in done:
            @pl.when(ended)
            def _():
                finalize_all(q)
        return carry

    qa0, ta0 = advance(zero, zero)
    carry = lax.fori_loop(0, total // steps, body,
                          (qa0, ta0, zero, zero, zero, jnp.int32(1)))
    pv_stage(carry[4], carry[5], (total - 1) % 2)
    finalize_all(carry[4])


def _attn_scratch(nh, n_kv, tk, n):
    return [pltpu.VMEM((nh, n_kv, DV_AUG, tk), BF16),
            pltpu.VMEM((nh, 2, tk, n), F32),
            pltpu.VMEM((nh, 2, tk, n), BF16),
            pltpu.VMEM((nh, 2, 1, n), F32),
            pltpu.VMEM((nh, 1, n), F32),
            pltpu.VMEM((nh, Q_SLOTS, DV_AUG, n), F32)]


def _fill_vt(vt_buf, hh, t, v):
    dv, tk = v.shape[1], v.shape[0]
    vt_buf[hh, t, :dv, :] = v.T
    row = lax.broadcasted_iota(jnp.int32, (DV_AUG - dv, tk), 0)
    vt_buf[hh, t, dv:, :] = jnp.where(row == 0, 1.0, 0.0).astype(BF16)


def _diff_attn_kernel(lam_ref, q_ref, k_ref, v_ref, gate_ref, tab_ref, g_ref, o_ref,
                      qz_buf, vt_buf, *scratch, out_scale):
    tq = TQ_DIFF
    hd = 2 * DIFF_HEAD_DIM
    assert hd == MLA_V_DIM
    n_q = q_ref.shape[1] // tq
    heads = range(NH_DIFF)
    lam = lam_ref[0, 0]

    def rows_of(i):
        return pl.ds(pl.multiple_of(i * tq, tq), tq)

    def cols_of(hh):
        return slice(hh * hd, (hh + 1) * hd)

    feat = lax.broadcasted_iota(jnp.int32, (hd, tq), 0)

    def prep(qi, carry):
        for hh in heads:
            qb = q_ref[0, rows_of(qi), cols_of(hh)].T
            zero = jnp.zeros_like(qb)
            qz_buf[hh, qi, :, :tq] = jnp.where(feat < DIFF_HEAD_DIM, qb, zero)
            qz_buf[hh, qi, :, tq:] = jnp.where(feat >= DIFF_HEAD_DIM, qb, zero)
            _fill_vt(vt_buf, hh, qi, v_ref[0, rows_of(qi), cols_of(hh)])
        return carry
    lax.fori_loop(0, n_q, prep, 0)

    def qk_operands(hh, qi, t):
        return k_ref[0, rows_of(t), cols_of(hh)], qz_buf[hh, qi]

    def vt_tile(hh, t):
        return vt_buf[hh, t]

    def bias_tile(hh, qi, t):
        return tab_ref[hh, jnp.maximum(t - qi + 2, 0)]

    def finalize(hh, qi, acc):
        inv = 1.0 / acc[hd:hd + 1]
        o_t = acc[:hd, :tq] * inv[:, :tq] - lam * (acc[:hd, tq:] * inv[:, tq:])
        o_t = o_t * lax.rsqrt(jnp.mean(o_t * o_t, axis=0, keepdims=True) + EPS)
        o = o_t.T * g_ref[...] * out_scale
        gate = gate_ref[0, rows_of(qi), cols_of(hh)].astype(F32)
        o_ref[0, rows_of(qi), cols_of(hh)] = (o * gate).astype(BF16)

    _attn_pipeline(n_q, STEPS_DIFF, heads, qk_operands, vt_tile, bias_tile, finalize, *scratch)


def _diff_attn(main3, lam, tab, g, lam_init):
    b, s, _ = main3.shape
    tq = TQ_DIFF
    hd = 2 * DIFF_HEAD_DIM
    nh = NH_DIFF
    blk = nh * hd
    kcol, vcol, gcol = DIFF_WIDTH // blk, 2 * DIFF_WIDTH // blk, QKV_WIDTH // blk
    seq_blk = lambda off: pl.BlockSpec((1, s, blk), lambda bi, h: (bi, 0, off + h))
    return pl.pallas_call(
        functools.partial(_diff_attn_kernel, out_scale=1.0 - lam_init),
        grid=(b, DIFF_HEADS // nh),
        in_specs=[
            pl.BlockSpec(memory_space=pltpu.SMEM),
            seq_blk(0), seq_blk(kcol), seq_blk(vcol), seq_blk(gcol),
            pl.BlockSpec((nh, 3, tq, 2 * tq), lambda bi, h: (h, 0, 0, 0)),
            pl.BlockSpec((1, hd), lambda bi, h: (0, 0)),
        ],
        out_specs=seq_blk(0),
        out_shape=jax.ShapeDtypeStruct((b, s, DIFF_WIDTH), BF16),
        scratch_shapes=[pltpu.VMEM((nh, s // tq, hd, 2 * tq), BF16)]
        + _attn_scratch(nh, s // tq, tq, 2 * tq),
        compiler_params=pltpu.CompilerParams(
            dimension_semantics=("parallel", "arbitrary"), vmem_limit_bytes=VMEM_LIMIT),
        name="diff_attn",
    )(lam, main3, main3, main3, main3, tab, g)


def _mla_attn_kernel(qn_ref, qr_ref, kv_ref, kr_ref, gate_ref, o_ref, qt_buf, vt_buf, *scratch):
    tq = TQ_MLA
    n_q = qn_ref.shape[1] // tq
    heads = range(NH_MLA)

    def rows_of(i):
        return pl.ds(pl.multiple_of(i * tq, tq), tq)

    def cols_of(hh):
        return slice(hh * LANE, (hh + 1) * LANE)

    def prep(qi, carry):
        for hh in heads:
            qt_buf[hh, qi, :LANE, :] = qn_ref[0, rows_of(qi), cols_of(hh)].T
            qt_buf[hh, qi, LANE:, :] = qr_ref[0, rows_of(qi), cols_of(hh)].T
            _fill_vt(vt_buf, hh, qi, kv_ref[0, rows_of(qi), cols_of(2 * hh + 1)])
        return carry
    lax.fori_loop(0, n_q, prep, 0)

    def qk_operands(hh, qi, t):
        kt = jnp.concatenate([kv_ref[0, rows_of(t), cols_of(2 * hh)], kr_ref[0, rows_of(t), :]],
                             axis=1)
        return kt, qt_buf[hh, qi]

    def vt_tile(hh, t):
        return vt_buf[hh, t]

    def finalize(hh, qi, acc):
        dv = MLA_V_DIM
        o = (acc[:dv] * (1.0 / acc[dv:dv + 1])).T
        gate = gate_ref[0, rows_of(qi), cols_of(hh)].astype(F32)
        o_ref[0, rows_of(qi), cols_of(hh)] = (o * gate).astype(BF16)

    _attn_pipeline(n_q, STEPS_MLA, heads, qk_operands, vt_tile, lambda hh, qi, t: None, finalize,
                   *scratch)


def _mla_attn(qn3, qr3, kv3, kr3, main3):
    b, s, _ = qn3.shape
    tq = TQ_MLA
    nh = NH_MLA
    blk = nh * LANE
    gcol = (QKV_WIDTH + DIFF_WIDTH) // blk
    seq_blk = lambda w, off: pl.BlockSpec((1, s, w), lambda bi, h: (bi, 0, off + h))
    return pl.pallas_call(
        _mla_attn_kernel,
        grid=(b, MLA_HEADS // nh),
        in_specs=[
            seq_blk(blk, 0), seq_blk(blk, 0), seq_blk(2 * blk, 0),
            pl.BlockSpec((1, s, LANE), lambda bi, h: (bi, 0, 0)),
            seq_blk(blk, gcol),
        ],
        out_specs=seq_blk(blk, 0),
        out_shape=jax.ShapeDtypeStruct((b, s, MLA_WIDTH), BF16),
        scratch_shapes=[pltpu.VMEM((nh, s // tq, 2 * LANE, tq), BF16)]
        + _attn_scratch(nh, s // tq, tq, tq),
        compiler_params=pltpu.CompilerParams(
            dimension_semantics=("parallel", "arbitrary"), vmem_limit_bytes=VMEM_LIMIT),
        name="mla_attn",
    )(qn3, qr3, kv3, kr3, main3)


def _out_proj_kernel(x_ref, ya_ref, yb_ref, w_ref, gf_ref, o_ref, *, final):
    acc = jnp.dot(ya_ref[...], w_ref[:DIFF_WIDTH, :], preferred_element_type=F32)
    acc = acc + jnp.dot(yb_ref[...], w_ref[DIFF_WIDTH:, :], preferred_element_type=F32)
    xn = x_ref[...] + acc
    if final:
        xn = xn * _rms_scale(xn) * gf_ref[...]
    o_ref[...] = xn


def _out_proj(x2, ya, yb, w_out, gf, final):
    m = x2.shape[0]
    row = lambda i: (i, 0)
    const = lambda i: (0, 0)
    return pl.pallas_call(
        functools.partial(_out_proj_kernel, final=final),
        grid=(m // TM_OUT,),
        in_specs=[
            pl.BlockSpec((TM_OUT, D_MODEL), row),
            pl.BlockSpec((TM_OUT, DIFF_WIDTH), row),
            pl.BlockSpec((TM_OUT, MLA_WIDTH), row),
            pl.BlockSpec((MIX_WIDTH, D_MODEL), const),
            pl.BlockSpec((1, D_MODEL), const),
        ],
        out_specs=pl.BlockSpec((TM_OUT, D_MODEL), row),
        out_shape=jax.ShapeDtypeStruct((m, D_MODEL), F32),
        compiler_params=pltpu.CompilerParams(
            dimension_semantics=("parallel",), vmem_limit_bytes=VMEM_LIMIT),
        name="out_proj",
    )(x2, ya, yb, w_out, gf)


def _rel_bucket(rel):
    nb = REL_BUCKETS // 2
    max_exact = nb // 2
    ret = (rel > 0).astype(jnp.int32) * nb
    n = jnp.abs(rel)
    nf = jnp.maximum(n, 1).astype(F32)
    large = max_exact + (jnp.log(nf / max_exact) / math.log(REL_MAX_DIST / max_exact)
                         * (nb - max_exact)).astype(jnp.int32)
    large = jnp.minimum(large, nb - 1)
    return ret + jnp.where(n < max_exact, n, large)


def _bias_tables(rel_bias):
    tq = TQ_DIFF
    kk = jnp.arange(2 * tq, dtype=jnp.int32)[:, None]
    qq = jnp.arange(tq, dtype=jnp.int32)[None, :]
    onehot = (_rel_bucket(kk - tq - qq)[..., None] == jnp.arange(REL_BUCKETS)).astype(F32)
    bias = jnp.einsum('kqb,bc->kqc', onehot, rel_bias.astype(F32),
                      precision=lax.Precision.HIGHEST) * LOG2E
    allowed = (kk < tq) | (((kk - tq) // CHUNK) <= (qq // CHUNK))
    bias = jnp.where(allowed[..., None], bias, NEG)
    near = bias.reshape(2, tq, tq, DIFF_HEADS, 2).transpose(3, 0, 1, 4, 2)
    near = near.reshape(DIFF_HEADS, 2, tq, 2 * tq)
    far = rel_bias[_rel_bucket(jnp.asarray(-2 * REL_MAX_DIST, jnp.int32))].astype(F32) * LOG2E
    far = jnp.repeat(far.reshape(DIFF_HEADS, 2), tq, axis=1)
    far = jnp.broadcast_to(far[:, None, None, :], (DIFF_HEADS, 1, tq, 2 * tq))
    return jnp.concatenate([far, near], axis=1)


def _pos_tables(seq):
    pos = jnp.arange(seq, dtype=F32)
    inv_freq = ROPE_BASE ** (-jnp.arange(0, MLA_ROPE, 2, dtype=F32) / MLA_ROPE)
    ang = pos[:, None] * inv_freq[None, :]
    cos, sin = jnp.cos(ang), jnp.sin(ang)
    pad = jnp.zeros((seq, LANE - MLA_ROPE), F32)
    n_chunks = seq // CHUNK
    assert n_chunks <= LANE - MLA_ROPE
    chunk = (jnp.arange(seq, dtype=jnp.int32) // CHUNK)[:, None]
    cid = jnp.arange(LANE, dtype=jnp.int32)[None, :] - MLA_ROPE
    valid = (cid >= 0) & (cid < n_chunks)
    mq = jnp.where(valid & (cid > chunk), NEG, 0.0).astype(F32)
    mk = jnp.where(valid & (cid == chunk), 1.0, 0.0).astype(F32)
    return (jnp.concatenate([cos, cos, pad], axis=1),
            jnp.concatenate([-sin, sin, pad], axis=1), mq, mk)


def _swap_halves(w):
    half = w.shape[-1] // 2
    return jnp.concatenate([w[..., half:], w[..., :half]], axis=-1)


def _pad_lanes(w):
    return jnp.pad(w, [(0, 0)] * (w.ndim - 1) + [(0, LANE - w.shape[-1])])


def kernel(x, norm_g, w_in, diff_lambda, diff_subln_g, mla_q_norm_g, mla_w_q_b,
           mla_kv_norm_g, mla_w_kv_b, w_out, rel_bias, final_norm_g):
    b, s, d = x.shape
    assert d == D_MODEL and s % TQ_MLA == 0 and s >= 2 * TQ_MLA and (b * s) % TM_PROJ == 0
    m = b * s
    pos_tables = _pos_tables(s)
    tab = _bias_tables(rel_bias)
    colscale = jnp.concatenate([
        jnp.full((1, DIFF_WIDTH), DIFF_HEAD_DIM ** -0.5 * LOG2E, F32),
        jnp.ones((1, MAIN_WIDTH - DIFF_WIDTH), F32)], axis=1)
    gf = final_norm_g.reshape(1, D_MODEL)

    x2 = x.reshape(m, d)
    for l in range(DEPTH):
        wl = w_in[l]
        w_main = jnp.concatenate([wl[:, :QKV_WIDTH], wl[:, GATE_OFF:]], axis=1).astype(BF16)
        w_kr = wl[:, GATE_OFF - MLA_ROPE:GATE_OFF]
        w_lat = jnp.concatenate([wl[:, LAT_OFF:GATE_OFF - MLA_ROPE], _pad_lanes(w_kr),
                                 _pad_lanes(_swap_halves(w_kr))], axis=1).astype(BF16)
        wq = mla_w_q_b[l].reshape(MLA_Q_LORA, MLA_HEADS, MLA_NOPE + MLA_ROPE)
        wq_rope = wq[:, :, MLA_NOPE:]
        w_q2 = jnp.concatenate([
            wq[:, :, :MLA_NOPE].reshape(MLA_Q_LORA, -1),
            _pad_lanes(wq_rope).reshape(MLA_Q_LORA, -1),
            _pad_lanes(_swap_halves(wq_rope)).reshape(MLA_Q_LORA, -1)], axis=1).astype(BF16)
        w_kv2 = mla_w_kv_b[l].astype(BF16)
        g = norm_g[l].reshape(1, D_MODEL)

        lam_init = 0.8 - 0.6 * math.exp(-0.3 * l)
        lp = diff_lambda[l].astype(F32)
        lam = jnp.exp(jnp.sum(lp[0] * lp[1])) - jnp.exp(jnp.sum(lp[2] * lp[3])) + lam_init

        main = _in_proj(x2, g, w_main, colscale)
        qn, qr, kv, kr = _latent(x2, g, w_lat, mla_q_norm_g[l].reshape(1, -1),
                                 mla_kv_norm_g[l].reshape(1, -1), w_q2, w_kv2, pos_tables, s)
        main3 = main.reshape(b, s, MAIN_WIDTH)
        ya = _diff_attn(main3, lam.reshape(1, 1), tab, diff_subln_g[l].reshape(1, -1), lam_init)
        yb = _mla_attn(qn.reshape(b, s, -1), qr.reshape(b, s, -1), kv.reshape(b, s, -1),
                       kr.reshape(b, s, -1), main3)
        x2 = _out_proj(x2, ya.reshape(m, -1), yb.reshape(m, -1), w_out[l].astype(BF16), gf,
                       final=(l == DEPTH - 1))
    return x2.reshape(b, s, d)
```

```python
import functools
import math

import jax
import jax.numpy as jnp
from jax import lax
from jax.experimental import pallas as pl
from jax.experimental.pallas import tpu as pltpu

D_MODEL = 2048
DEPTH = 2
CHUNK = 64
MIX_WIDTH = D_MODEL
DIFF_WIDTH = MIX_WIDTH // 2
MLA_WIDTH = MIX_WIDTH - DIFF_WIDTH
DIFF_HEAD_DIM = 64
DIFF_HEADS = DIFF_WIDTH // (2 * DIFF_HEAD_DIM)
MLA_V_DIM = 128
MLA_HEADS = MLA_WIDTH // MLA_V_DIM
MLA_NOPE = 128
MLA_ROPE = 64
MLA_Q_LORA = 512
MLA_KV_LORA = 256
ROPE_BASE = 10000.0
REL_BUCKETS = 32
REL_MAX_DIST = 128
EPS = 1e-6
NEG = -1e30
QKV_WIDTH = 3 * DIFF_WIDTH
LAT_OFF = QKV_WIDTH
GATE_OFF = LAT_OFF + MLA_Q_LORA + MLA_KV_LORA + MLA_ROPE
MAIN_WIDTH = QKV_WIDTH + MIX_WIDTH

LOG2E = math.log2(math.e)
LANE = 128
VMEM_LIMIT = 56 * 1024 * 1024

BF16 = jnp.bfloat16
F32 = jnp.float32

TM_PROJ = 1024
TN_PROJ = 1024
RC_NORM = 256
TM_LAT = 512
TM_OUT = 512
TQ_DIFF = 256
TQ_MLA = 512
NH_DIFF = 1
NH_MLA = 1
STEPS_DIFF = 8
STEPS_MLA = 4
Q_SLOTS = 4
QK_LEAD = 2
S_SLOTS = 4

DV_AUG = MLA_V_DIM + 16


def _rms_scale(xf):
    return lax.rsqrt(jnp.mean(xf * xf, axis=-1, keepdims=True) + EPS)


def _in_proj_kernel(x_ref, g_ref, w_ref, cs_ref, o_ref, h_ref, *, n_plain):
    j = pl.program_id(1)

    @pl.when(j == 0)
    def _():
        def body(c, carry):
            rows = pl.ds(pl.multiple_of(c * RC_NORM, RC_NORM), RC_NORM)
            xf = x_ref[rows, :]
            h_ref[rows, :] = (xf * _rms_scale(xf) * g_ref[...]).astype(BF16)
            return carry
        lax.fori_loop(0, TM_PROJ // RC_NORM, body, 0)

    acc = jnp.dot(h_ref[...], w_ref[...], preferred_element_type=F32) * cs_ref[...]

    @pl.when(j < n_plain)
    def _():
        o_ref[...] = acc.astype(BF16)

    @pl.when(j >= n_plain)
    def _():
        o_ref[...] = (acc * jax.nn.sigmoid(acc)).astype(BF16)


def _in_proj(x2, g, w_main, colscale):
    m = x2.shape[0]
    n = w_main.shape[1]
    return pl.pallas_call(
        functools.partial(_in_proj_kernel, n_plain=QKV_WIDTH // TN_PROJ),
        grid=(m // TM_PROJ, n // TN_PROJ),
        in_specs=[
            pl.BlockSpec((TM_PROJ, D_MODEL), lambda i, j: (i, 0)),
            pl.BlockSpec((1, D_MODEL), lambda i, j: (0, 0)),
            pl.BlockSpec((D_MODEL, TN_PROJ), lambda i, j: (0, j)),
            pl.BlockSpec((1, TN_PROJ), lambda i, j: (0, j)),
        ],
        out_specs=pl.BlockSpec((TM_PROJ, TN_PROJ), lambda i, j: (i, j)),
        out_shape=jax.ShapeDtypeStruct((m, n), BF16),
        scratch_shapes=[pltpu.VMEM((TM_PROJ, D_MODEL), BF16)],
        compiler_params=pltpu.CompilerParams(
            dimension_semantics=("parallel", "arbitrary"), vmem_limit_bytes=VMEM_LIMIT),
        name="in_proj",
    )(x2, g, w_main, colscale)


def _latent_kernel(x_ref, g_ref, wl_ref, gq_ref, gkv_ref, wq_ref, wkv_ref, cos_ref, sin_ref,
                   mq_ref, mk_ref, qn_ref, qr_ref, kv_ref, kr_ref, *, q_scale):
    xf = x_ref[...]
    h = (xf * _rms_scale(xf) * g_ref[...]).astype(BF16)
    lat = jnp.dot(h, wl_ref[...], preferred_element_type=F32)
    cq = lat[:, :MLA_Q_LORA]
    ckv = lat[:, MLA_Q_LORA:MLA_Q_LORA + MLA_KV_LORA]
    ka = lat[:, MLA_Q_LORA + MLA_KV_LORA:MLA_Q_LORA + MLA_KV_LORA + LANE]
    kb = lat[:, MLA_Q_LORA + MLA_KV_LORA + LANE:]
    cos = cos_ref[...]
    sin = sin_ref[...]
    mq = mq_ref[...]

    cqn = (cq * _rms_scale(cq) * gq_ref[...]).astype(BF16)
    qall = jnp.dot(cqn, wq_ref[...], preferred_element_type=F32) * q_scale
    nope_w = MLA_HEADS * MLA_NOPE
    qn_ref[...] = qall[:, :nope_w].astype(BF16)
    for hh in range(MLA_HEADS):
        a = qall[:, nope_w + hh * LANE:nope_w + (hh + 1) * LANE]
        b = qall[:, 2 * nope_w + hh * LANE:2 * nope_w + (hh + 1) * LANE]
        qr_ref[:, hh * LANE:(hh + 1) * LANE] = (a * cos + b * sin + mq).astype(BF16)

    ckvn = (ckv * _rms_scale(ckv) * gkv_ref[...]).astype(BF16)
    kv_ref[...] = jnp.dot(ckvn, wkv_ref[...], preferred_element_type=F32).astype(BF16)
    kr_ref[...] = (ka * cos + kb * sin + mk_ref[...]).astype(BF16)


def _latent(x2, g, w_lat, gq, gkv, w_q2, w_kv2, pos_tables, seq):
    m = x2.shape[0]
    seq_tiles = seq // TM_LAT
    const = lambda i: (0, 0)
    row = lambda i: (i, 0)
    pos = lambda i: (i % seq_tiles, 0)
    q_scale = (MLA_NOPE + MLA_ROPE) ** -0.5 * LOG2E
    return pl.pallas_call(
        functools.partial(_latent_kernel, q_scale=q_scale),
        grid=(m // TM_LAT,),
        in_specs=[
            pl.BlockSpec((TM_LAT, D_MODEL), row),
            pl.BlockSpec((1, D_MODEL), const),
            pl.BlockSpec(w_lat.shape, const),
            pl.BlockSpec((1, MLA_Q_LORA), const),
            pl.BlockSpec((1, MLA_KV_LORA), const),
            pl.BlockSpec(w_q2.shape, const),
            pl.BlockSpec(w_kv2.shape, const),
        ] + [pl.BlockSpec((TM_LAT, LANE), pos)] * len(pos_tables),
        out_specs=[
            pl.BlockSpec((TM_LAT, MLA_HEADS * LANE), row),
            pl.BlockSpec((TM_LAT, MLA_HEADS * LANE), row),
            pl.BlockSpec((TM_LAT, w_kv2.shape[1]), row),
            pl.BlockSpec((TM_LAT, LANE), row),
        ],
        out_shape=[
            jax.ShapeDtypeStruct((m, MLA_HEADS * LANE), BF16),
            jax.ShapeDtypeStruct((m, MLA_HEADS * LANE), BF16),
            jax.ShapeDtypeStruct((m, w_kv2.shape[1]), BF16),
            jax.ShapeDtypeStruct((m, LANE), BF16),
        ],
        compiler_params=pltpu.CompilerParams(
            dimension_semantics=("parallel",), vmem_limit_bytes=VMEM_LIMIT),
        name="latent_proj",
    )(x2, g, w_lat, gq, gkv, w_q2, w_kv2, *pos_tables)


def _attn_pipeline(n_q, steps, heads, qk_operands, vt_tile, bias_tile, finalize,
                   s_buf, p_buf, a_buf, m_ref, acc_ref):
    last = n_q - 1
    total = n_q * (n_q + 1) // 2

    def advance(q, t):
        end = t >= q
        return (jnp.where(end, jnp.minimum(q + 1, last), q),
                jnp.where(end, jnp.where(q >= last, t, 0), t + 1))

    def qk_stage(qi, t, slot):
        for hh in heads:
            kt, qt = qk_operands(hh, qi, t)
            s_buf[hh, slot] = jnp.dot(kt, qt, preferred_element_type=F32)

    def sm_stage(qi, t, s_slot, p_slot):
        for hh in heads:
            s = s_buf[hh, s_slot]
            bias = bias_tile(hh, qi, t)
            if bias is not None:
                s = s + bias
            m_old = jnp.where(t == 0, NEG, m_ref[hh])
            m_new = jnp.maximum(m_old, jnp.max(s, axis=0, keepdims=True))
            m_ref[hh] = m_new
            p_buf[hh, p_slot] = jnp.exp2(s - m_new).astype(BF16)
            a_buf[hh, p_slot] = jnp.exp2(m_old - m_new)

    def pv_stage(qi, t, p_slot):
        for hh in heads:
            pv = jnp.dot(vt_tile(hh, t), p_buf[hh, p_slot], preferred_element_type=F32)
            qs = qi & (Q_SLOTS - 1)
            acc_ref[hh, qs] = a_buf[hh, p_slot] * acc_ref[hh, qs] + pv

    def finalize_all(qi):
        for hh in heads:
            finalize(hh, qi, acc_ref[hh, qi & (Q_SLOTS - 1)])

    m_ref[...] = jnp.full(m_ref.shape, NEG, F32)
    acc_ref[...] = jnp.zeros(acc_ref.shape, F32)
    p_buf[...] = jnp.zeros(p_buf.shape, BF16)
    a_buf[...] = jnp.zeros(a_buf.shape, F32)

    zero = jnp.int32(0)
    ahead = [(zero, zero)]
    for _ in range(QK_LEAD):
        ahead.append(advance(*ahead[-1]))
    for i in range(QK_LEAD):
        qk_stage(*ahead[i], i)

    def step(carry, j):
        lead, cur, prv = carry[:QK_LEAD], carry[QK_LEAD], carry[QK_LEAD + 1]
        pv_stage(*prv, (j + 1) % 2)
        qk_stage(*lead[0], (j + QK_LEAD) % S_SLOTS)
        sm_stage(*cur, j % S_SLOTS, j % 2)
        return (advance(*lead[0]),) + lead + (cur,), (prv[1] == prv[0], prv[0])

    assert total % steps == 0 and steps % S_SLOTS == 0 and steps <= 8 and Q_SLOTS == 4

    def body(_, carry):
        done = []
        for j in range(steps):
            carry, d = step(carry, j)
            done.append(d)
        for ended, q in done:
            @pl.when(ended)
            def _():
                finalize_all(q)
        return carry

    carry = tuple(reversed(ahead)) + ((zero, jnp.int32(1)),)
    carry = lax.fori_loop(0, total // steps, body, carry)
    pv_stage(*carry[-1], (total - 1) % 2)
    finalize_all(carry[-1][0])


def _attn_scratch(nh, n_kv, tk, n):
    return [pltpu.VMEM((nh, n_kv, DV_AUG, tk), BF16),
            pltpu.VMEM((nh, S_SLOTS, tk, n), F32),
            pltpu.VMEM((nh, 2, tk, n), BF16),
            pltpu.VMEM((nh, 2, 1, n), F32),
            pltpu.VMEM((nh, 1, n), F32),
            pltpu.VMEM((nh, Q_SLOTS, DV_AUG, n), F32)]


def _fill_vt(vt_buf, hh, t, v):
    dv, tk = v.shape[1], v.shape[0]
    vt_buf[hh, t, :dv, :] = v.T
    row = lax.broadcasted_iota(jnp.int32, (DV_AUG - dv, tk), 0)
    vt_buf[hh, t, dv:, :] = jnp.where(row == 0, 1.0, 0.0).astype(BF16)


def _diff_attn_kernel(lam_ref, q_ref, k_ref, v_ref, gate_ref, tab_ref, g_ref, o_ref,
                      qz_buf, vt_buf, *scratch, out_scale):
    tq = TQ_DIFF
    hd = 2 * DIFF_HEAD_DIM
    assert hd == MLA_V_DIM
    n_q = q_ref.shape[1] // tq
    heads = range(NH_DIFF)
    lam = lam_ref[0, 0]

    def rows_of(i):
        return pl.ds(pl.multiple_of(i * tq, tq), tq)

    def cols_of(hh):
        return slice(hh * hd, (hh + 1) * hd)

    feat = lax.broadcasted_iota(jnp.int32, (hd, tq), 0)

    def prep(qi, carry):
        for hh in heads:
            qb = q_ref[0, rows_of(qi), cols_of(hh)].T
            zero = jnp.zeros_like(qb)
            qz_buf[hh, qi, :, :tq] = jnp.where(feat < DIFF_HEAD_DIM, qb, zero)
            qz_buf[hh, qi, :, tq:] = jnp.where(feat >= DIFF_HEAD_DIM, qb, zero)
            _fill_vt(vt_buf, hh, qi, v_ref[0, rows_of(qi), cols_of(hh)])
        return carry
    lax.fori_loop(0, n_q, prep, 0)

    def qk_operands(hh, qi, t):
        return k_ref[0, rows_of(t), cols_of(hh)], qz_buf[hh, qi]

    def vt_tile(hh, t):
        return vt_buf[hh, t]

    def bias_tile(hh, qi, t):
        return tab_ref[hh, jnp.maximum(t - qi + 2, 0)]

    def finalize(hh, qi, acc):
        inv = 1.0 / acc[hd:hd + 1]
        o_t = acc[:hd, :tq] * inv[:, :tq] - lam * (acc[:hd, tq:] * inv[:, tq:])
        o_t = o_t * lax.rsqrt(jnp.mean(o_t * o_t, axis=0, keepdims=True) + EPS)
        o = o_t.T * g_ref[...] * out_scale
        gate = gate_ref[0, rows_of(qi), cols_of(hh)].astype(F32)
        o_ref[0, rows_of(qi), cols_of(hh)] = (o * gate).astype(BF16)

    _attn_pipeline(n_q, STEPS_DIFF, heads, qk_operands, vt_tile, bias_tile, finalize, *scratch)


def _diff_attn(main3, lam, tab, g, lam_init):
    b, s, _ = main3.shape
    tq = TQ_DIFF
    hd = 2 * DIFF_HEAD_DIM
    nh = NH_DIFF
    blk = nh * hd
    kcol, vcol, gcol = DIFF_WIDTH // blk, 2 * DIFF_WIDTH // blk, QKV_WIDTH // blk
    seq_blk = lambda off: pl.BlockSpec((1, s, blk), lambda bi, h: (bi, 0, off + h))
    return pl.pallas_call(
        functools.partial(_diff_attn_kernel, out_scale=1.0 - lam_init),
        grid=(b, DIFF_HEADS // nh),
        in_specs=[
            pl.BlockSpec(memory_space=pltpu.SMEM),
            seq_blk(0), seq_blk(kcol), seq_blk(vcol), seq_blk(gcol),
            pl.BlockSpec((nh, 3, tq, 2 * tq), lambda bi, h: (h, 0, 0, 0)),
            pl.BlockSpec((1, hd), lambda bi, h: (0, 0)),
        ],
        out_specs=seq_blk(0),
        out_shape=jax.ShapeDtypeStruct((b, s, DIFF_WIDTH), BF16),
        scratch_shapes=[pltpu.VMEM((nh, s // tq, hd, 2 * tq), BF16)]
        + _attn_scratch(nh, s // tq, tq, 2 * tq),
        compiler_params=pltpu.CompilerParams(
            dimension_semantics=("parallel", "arbitrary"), vmem_limit_bytes=VMEM_LIMIT),
        name="diff_attn",
    )(lam, main3, main3, main3, main3, tab, g)


def _mla_attn_kernel(qn_ref, qr_ref, kv_ref, kr_ref, gate_ref, o_ref, qt_buf, vt_buf, *scratch):
    tq = TQ_MLA
    n_q = qn_ref.shape[1] // tq
    heads = range(NH_MLA)

    def rows_of(i):
        return pl.ds(pl.multiple_of(i * tq, tq), tq)

    def cols_of(hh):
        return slice(hh * LANE, (hh + 1) * LANE)

    def prep(qi, carry):
        for hh in heads:
            qt_buf[hh, qi, :LANE, :] = qn_ref[0, rows_of(qi), cols_of(hh)].T
            qt_buf[hh, qi, LANE:, :] = qr_ref[0, rows_of(qi), cols_of(hh)].T
            _fill_vt(vt_buf, hh, qi, kv_ref[0, rows_of(qi), cols_of(2 * hh + 1)])
        return carry
    lax.fori_loop(0, n_q, prep, 0)

    def qk_operands(hh, qi, t):
        kt = jnp.concatenate([kv_ref[0, rows_of(t), cols_of(2 * hh)], kr_ref[0, rows_of(t), :]],
                             axis=1)
        return kt, qt_buf[hh, qi]

    def vt_tile(hh, t):
        return vt_buf[hh, t]

    def finalize(hh, qi, acc):
        dv = MLA_V_DIM
        o = (acc[:dv] * (1.0 / acc[dv:dv + 1])).T
        gate = gate_ref[0, rows_of(qi), cols_of(hh)].astype(F32)
        o_ref[0, rows_of(qi), cols_of(hh)] = (o * gate).astype(BF16)

    _attn_pipeline(n_q, STEPS_MLA, heads, qk_operands, vt_tile, lambda hh, qi, t: None, finalize,
                   *scratch)


def _mla_attn(qn3, qr3, kv3, kr3, main3):
    b, s, _ = qn3.shape
    tq = TQ_MLA
    nh = NH_MLA
    blk = nh * LANE
    gcol = (QKV_WIDTH + DIFF_WIDTH) // blk
    seq_blk = lambda w, off: pl.BlockSpec((1, s, w), lambda bi, h: (bi, 0, off + h))
    return pl.pallas_call(
        _mla_attn_kernel,
        grid=(b, MLA_HEADS // nh),
        in_specs=[
            seq_blk(blk, 0), seq_blk(blk, 0), seq_blk(2 * blk, 0),
            pl.BlockSpec((1, s, LANE), lambda bi, h: (bi, 0, 0)),
            seq_blk(blk, gcol),
        ],
        out_specs=seq_blk(blk, 0),
        out_shape=jax.ShapeDtypeStruct((b, s, MLA_WIDTH), BF16),
        scratch_shapes=[pltpu.VMEM((nh, s // tq, 2 * LANE, tq), BF16)]
        + _attn_scratch(nh, s // tq, tq, tq),
        compiler_params=pltpu.CompilerParams(
            dimension_semantics=("parallel", "arbitrary"), vmem_limit_bytes=VMEM_LIMIT),
        name="mla_attn",
    )(qn3, qr3, kv3, kr3, main3)


def _out_proj_kernel(x_ref, ya_ref, yb_ref, w_ref, gf_ref, o_ref, *, final):
    acc = jnp.dot(ya_ref[...], w_ref[:DIFF_WIDTH, :], preferred_element_type=F32)
    acc = acc + jnp.dot(yb_ref[...], w_ref[DIFF_WIDTH:, :], preferred_element_type=F32)
    xn = x_ref[...] + acc
    if final:
        xn = xn * _rms_scale(xn) * gf_ref[...]
    o_ref[...] = xn


def _out_proj(x2, ya, yb, w_out, gf, final):
    m = x2.shape[0]
    row = lambda i: (i, 0)
    const = lambda i: (0, 0)
    return pl.pallas_call(
        functools.partial(_out_proj_kernel, final=final),
        grid=(m // TM_OUT,),
        in_specs=[
            pl.BlockSpec((TM_OUT, D_MODEL), row),
            pl.BlockSpec((TM_OUT, DIFF_WIDTH), row),
            pl.BlockSpec((TM_OUT, MLA_WIDTH), row),
            pl.BlockSpec((MIX_WIDTH, D_MODEL), const),
            pl.BlockSpec((1, D_MODEL), const),
        ],
        out_specs=pl.BlockSpec((TM_OUT, D_MODEL), row),
        out_shape=jax.ShapeDtypeStruct((m, D_MODEL), F32),
        compiler_params=pltpu.CompilerParams(
            dimension_semantics=("parallel",), vmem_limit_bytes=VMEM_LIMIT),
        name="out_proj",
    )(x2, ya, yb, w_out, gf)


def _rel_bucket(rel):
    nb = REL_BUCKETS // 2
    max_exact = nb // 2
    ret = (rel > 0).astype(jnp.int32) * nb
    n = jnp.abs(rel)
    nf = jnp.maximum(n, 1).astype(F32)
    large = max_exact + (jnp.log(nf / max_exact) / math.log(REL_MAX_DIST / max_exact)
                         * (nb - max_exact)).astype(jnp.int32)
    large = jnp.minimum(large, nb - 1)
    return ret + jnp.where(n < max_exact, n, large)


def _bias_tables(rel_bias):
    tq = TQ_DIFF
    kk = jnp.arange(2 * tq, dtype=jnp.int32)[:, None]
    qq = jnp.arange(tq, dtype=jnp.int32)[None, :]
    onehot = (_rel_bucket(kk - tq - qq)[..., None] == jnp.arange(REL_BUCKETS)).astype(F32)
    bias = jnp.einsum('kqb,bc->kqc', onehot, rel_bias.astype(F32),
                      precision=lax.Precision.HIGHEST) * LOG2E
    allowed = (kk < tq) | (((kk - tq) // CHUNK) <= (qq // CHUNK))
    bias = jnp.where(allowed[..., None], bias, NEG)
    near = bias.reshape(2, tq, tq, DIFF_HEADS, 2).transpose(3, 0, 1, 4, 2)
    near = near.reshape(DIFF_HEADS, 2, tq, 2 * tq)
    far = rel_bias[_rel_bucket(jnp.asarray(-2 * REL_MAX_DIST, jnp.int32))].astype(F32) * LOG2E
    far = jnp.repeat(far.reshape(DIFF_HEADS, 2), tq, axis=1)
    far = jnp.broadcast_to(far[:, None, None, :], (DIFF_HEADS, 1, tq, 2 * tq))
    return jnp.concatenate([far, near], axis=1)


def _pos_tables(seq):
    pos = jnp.arange(seq, dtype=F32)
    inv_freq = ROPE_BASE ** (-jnp.arange(0, MLA_ROPE, 2, dtype=F32) / MLA_ROPE)
    ang = pos[:, None] * inv_freq[None, :]
    cos, sin = jnp.cos(ang), jnp.sin(ang)
    pad = jnp.zeros((seq, LANE - MLA_ROPE), F32)
    n_chunks = seq // CHUNK
    assert n_chunks <= LANE - MLA_ROPE
    chunk = (jnp.arange(seq, dtype=jnp.int32) // CHUNK)[:, None]
    cid = jnp.arange(LANE, dtype=jnp.int32)[None, :] - MLA_ROPE
    valid = (cid >= 0) & (cid < n_chunks)
    mq = jnp.where(valid & (cid > chunk), NEG, 0.0).astype(F32)
    mk = jnp.where(valid & (cid == chunk), 1.0, 0.0).astype(F32)
    return (jnp.concatenate([cos, cos, pad], axis=1),
            jnp.concatenate([-sin, sin, pad], axis=1), mq, mk)


def _swap_halves(w):
    half = w.shape[-1] // 2
    return jnp.concatenate([w[..., half:], w[..., :half]], axis=-1)


def _pad_lanes(w):
    return jnp.pad(w, [(0, 0)] * (w.ndim - 1) + [(0, LANE - w.shape[-1])])


def kernel(x, norm_g, w_in, diff_lambda, diff_subln_g, mla_q_norm_g, mla_w_q_b,
           mla_kv_norm_g, mla_w_kv_b, w_out, rel_bias, final_norm_g):
    b, s, d = x.shape
    assert d == D_MODEL and s % TQ_MLA == 0 and s >= 2 * TQ_MLA and (b * s) % TM_PROJ == 0
    m = b * s
    pos_tables = _pos_tables(s)
    tab = _bias_tables(rel_bias)
    colscale = jnp.concatenate([
        jnp.full((1, DIFF_WIDTH), DIFF_HEAD_DIM ** -0.5 * LOG2E, F32),
        jnp.ones((1, MAIN_WIDTH - DIFF_WIDTH), F32)], axis=1)
    gf = final_norm_g.reshape(1, D_MODEL)

    x2 = x.reshape(m, d)
    for l in range(DEPTH):
        wl = w_in[l]
        w_main = jnp.concatenate([wl[:, :QKV_WIDTH], wl[:, GATE_OFF:]], axis=1).astype(BF16)
        w_kr = wl[:, GATE_OFF - MLA_ROPE:GATE_OFF]
        w_lat = jnp.concatenate([wl[:, LAT_OFF:GATE_OFF - MLA_ROPE], _pad_lanes(w_kr),
                                 _pad_lanes(_swap_halves(w_kr))], axis=1).astype(BF16)
        wq = mla_w_q_b[l].reshape(MLA_Q_LORA, MLA_HEADS, MLA_NOPE + MLA_ROPE)
        wq_rope = wq[:, :, MLA_NOPE:]
        w_q2 = jnp.concatenate([
            wq[:, :, :MLA_NOPE].reshape(MLA_Q_LORA, -1),
            _pad_lanes(wq_rope).reshape(MLA_Q_LORA, -1),
            _pad_lanes(_swap_halves(wq_rope)).reshape(MLA_Q_LORA, -1)], axis=1).astype(BF16)
        w_kv2 = mla_w_kv_b[l].astype(BF16)
        g = norm_g[l].reshape(1, D_MODEL)

        lam_init = 0.8 - 0.6 * math.exp(-0.3 * l)
        lp = diff_lambda[l].astype(F32)
        lam = jnp.exp(jnp.sum(lp[0] * lp[1])) - jnp.exp(jnp.sum(lp[2] * lp[3])) + lam_init

        main = _in_proj(x2, g, w_main, colscale)
        qn, qr, kv, kr = _latent(x2, g, w_lat, mla_q_norm_g[l].reshape(1, -1),
                                 mla_kv_norm_g[l].reshape(1, -1), w_q2, w_kv2, pos_tables, s)
        main3 = main.reshape(b, s, MAIN_WIDTH)
        ya = _diff_attn(main3, lam.reshape(1, 1), tab, diff_subln_g[l].reshape(1, -1), lam_init)
        yb = _mla_attn(qn.reshape(b, s, -1), qr.reshape(b, s, -1), kv.reshape(b, s, -1),
                       kr.reshape(b, s, -1), main3)
        x2 = _out_proj(x2, ya.reshape(m, -1), yb.reshape(m, -1), w_out[l].astype(BF16), gf,
                       final=(l == DEPTH - 1))
    return x2.reshape(b, s, d)
```

```python
import functools
import math

import jax
import jax.numpy as jnp
from jax import lax
from jax.experimental import pallas as pl
from jax.experimental.pallas import tpu as pltpu

D_MODEL = 2048
DEPTH = 2
CHUNK = 64
MIX_WIDTH = D_MODEL
DIFF_WIDTH = MIX_WIDTH // 2
MLA_WIDTH = MIX_WIDTH - DIFF_WIDTH
DIFF_HEAD_DIM = 64
DIFF_HEADS = DIFF_WIDTH // (2 * DIFF_HEAD_DIM)
MLA_V_DIM = 128
MLA_HEADS = MLA_WIDTH // MLA_V_DIM
MLA_NOPE = 128
MLA_ROPE = 64
MLA_Q_LORA = 512
MLA_KV_LORA = 256
ROPE_BASE = 10000.0
REL_BUCKETS = 32
REL_MAX_DIST = 128
EPS = 1e-6
NEG = -1e30
QKV_WIDTH = 3 * DIFF_WIDTH
LAT_OFF = QKV_WIDTH
GATE_OFF = LAT_OFF + MLA_Q_LORA + MLA_KV_LORA + MLA_ROPE
MAIN_WIDTH = QKV_WIDTH + MIX_WIDTH

LOG2E = math.log2(math.e)
LANE = 128
VMEM_LIMIT = 56 * 1024 * 1024

BF16 = jnp.bfloat16
F32 = jnp.float32

TM_PROJ = 1024
TN_PROJ = 1024
RC_NORM = 256
TM_LAT = 512
TM_OUT = 512
TQ_DIFF = 256
TQ_MLA = 512
NH_DIFF = 1
NH_MLA = 1
TRIPS_DIFF = ((11, 12), (1, 4))
TRIPS_MLA = ((3, 12),)
Q_SLOTS = 8
QK_LEAD = 2
S_SLOTS = 4

DV_AUG = MLA_V_DIM + 16


def _rms_scale(xf):
    return lax.rsqrt(jnp.mean(xf * xf, axis=-1, keepdims=True) + EPS)


def _in_proj_kernel(x_ref, g_ref, w_ref, cs_ref, o_ref, h_ref, *, n_plain):
    j = pl.program_id(1)

    @pl.when(j == 0)
    def _():
        def body(c, carry):
            rows = pl.ds(pl.multiple_of(c * RC_NORM, RC_NORM), RC_NORM)
            xf = x_ref[rows, :]
            h_ref[rows, :] = (xf * _rms_scale(xf) * g_ref[...]).astype(BF16)
            return carry
        lax.fori_loop(0, TM_PROJ // RC_NORM, body, 0)

    acc = jnp.dot(h_ref[...], w_ref[...], preferred_element_type=F32) * cs_ref[...]

    @pl.when(j < n_plain)
    def _():
        o_ref[...] = acc.astype(BF16)

    @pl.when(j >= n_plain)
    def _():
        o_ref[...] = (acc * jax.nn.sigmoid(acc)).astype(BF16)


def _in_proj(x2, g, w_main, colscale):
    m = x2.shape[0]
    n = w_main.shape[1]
    return pl.pallas_call(
        functools.partial(_in_proj_kernel, n_plain=QKV_WIDTH // TN_PROJ),
        grid=(m // TM_PROJ, n // TN_PROJ),
        in_specs=[
            pl.BlockSpec((TM_PROJ, D_MODEL), lambda i, j: (i, 0)),
            pl.BlockSpec((1, D_MODEL), lambda i, j: (0, 0)),
            pl.BlockSpec((D_MODEL, TN_PROJ), lambda i, j: (0, j)),
            pl.BlockSpec((1, TN_PROJ), lambda i, j: (0, j)),
        ],
        out_specs=pl.BlockSpec((TM_PROJ, TN_PROJ), lambda i, j: (i, j)),
        out_shape=jax.ShapeDtypeStruct((m, n), BF16),
        scratch_shapes=[pltpu.VMEM((TM_PROJ, D_MODEL), BF16)],
        compiler_params=pltpu.CompilerParams(
            dimension_semantics=("parallel", "arbitrary"), vmem_limit_bytes=VMEM_LIMIT),
        name="in_proj",
    )(x2, g, w_main, colscale)


def _latent_kernel(x_ref, g_ref, wl_ref, gq_ref, gkv_ref, wq_ref, wkv_ref, cos_ref, sin_ref,
                   mq_ref, mk_ref, qn_ref, qr_ref, kv_ref, kr_ref, *, q_scale):
    xf = x_ref[...]
    h = (xf * _rms_scale(xf) * g_ref[...]).astype(BF16)
    lat = jnp.dot(h, wl_ref[...], preferred_element_type=F32)
    cq = lat[:, :MLA_Q_LORA]
    ckv = lat[:, MLA_Q_LORA:MLA_Q_LORA + MLA_KV_LORA]
    ka = lat[:, MLA_Q_LORA + MLA_KV_LORA:MLA_Q_LORA + MLA_KV_LORA + LANE]
    kb = lat[:, MLA_Q_LORA + MLA_KV_LORA + LANE:]
    cos = cos_ref[...]
    sin = sin_ref[...]
    mq = mq_ref[...]

    cqn = (cq * _rms_scale(cq) * gq_ref[...]).astype(BF16)
    qall = jnp.dot(cqn, wq_ref[...], preferred_element_type=F32) * q_scale
    nope_w = MLA_HEADS * MLA_NOPE
    qn_ref[...] = qall[:, :nope_w].astype(BF16)
    for hh in range(MLA_HEADS):
        a = qall[:, nope_w + hh * LANE:nope_w + (hh + 1) * LANE]
        b = qall[:, 2 * nope_w + hh * LANE:2 * nope_w + (hh + 1) * LANE]
        qr_ref[:, hh * LANE:(hh + 1) * LANE] = (a * cos + b * sin + mq).astype(BF16)

    ckvn = (ckv * _rms_scale(ckv) * gkv_ref[...]).astype(BF16)
    kv_ref[...] = jnp.dot(ckvn, wkv_ref[...], preferred_element_type=F32).astype(BF16)
    kr_ref[...] = (ka * cos + kb * sin + mk_ref[...]).astype(BF16)


def _latent(x2, g, w_lat, gq, gkv, w_q2, w_kv2, pos_tables, seq):
    m = x2.shape[0]
    seq_tiles = seq // TM_LAT
    const = lambda i: (0, 0)
    row = lambda i: (i, 0)
    pos = lambda i: (i % seq_tiles, 0)
    q_scale = (MLA_NOPE + MLA_ROPE) ** -0.5 * LOG2E
    return pl.pallas_call(
        functools.partial(_latent_kernel, q_scale=q_scale),
        grid=(m // TM_LAT,),
        in_specs=[
            pl.BlockSpec((TM_LAT, D_MODEL), row),
            pl.BlockSpec((1, D_MODEL), const),
            pl.BlockSpec(w_lat.shape, const),
            pl.BlockSpec((1, MLA_Q_LORA), const),
            pl.BlockSpec((1, MLA_KV_LORA), const),
            pl.BlockSpec(w_q2.shape, const),
            pl.BlockSpec(w_kv2.shape, const),
        ] + [pl.BlockSpec((TM_LAT, LANE), pos)] * len(pos_tables),
        out_specs=[
            pl.BlockSpec((TM_LAT, MLA_HEADS * LANE), row),
            pl.BlockSpec((TM_LAT, MLA_HEADS * LANE), row),
            pl.BlockSpec((TM_LAT, w_kv2.shape[1]), row),
            pl.BlockSpec((TM_LAT, LANE), row),
        ],
        out_shape=[
            jax.ShapeDtypeStruct((m, MLA_HEADS * LANE), BF16),
            jax.ShapeDtypeStruct((m, MLA_HEADS * LANE), BF16),
            jax.ShapeDtypeStruct((m, w_kv2.shape[1]), BF16),
            jax.ShapeDtypeStruct((m, LANE), BF16),
        ],
        compiler_params=pltpu.CompilerParams(
            dimension_semantics=("parallel",), vmem_limit_bytes=VMEM_LIMIT),
        name="latent_proj",
    )(x2, g, w_lat, gq, gkv, w_q2, w_kv2, *pos_tables)


def _attn_pipeline(n_q, trip_plan, heads, qk_operands, vt_tile, bias_tile, finalize,
                   s_buf, p_buf, a_buf, m_ref, acc_ref):
    last = n_q - 1
    total = n_q * (n_q + 1) // 2

    def advance(q, t):
        end = t >= q
        return (jnp.where(end, jnp.minimum(q + 1, last), q),
                jnp.where(end, jnp.where(q >= last, t, 0), t + 1))

    def qk_stage(qi, t, slot):
        for hh in heads:
            kt, qt = qk_operands(hh, qi, t)
            s_buf[hh, slot] = jnp.dot(kt, qt, preferred_element_type=F32)

    def sm_stage(qi, t, s_slot, p_slot):
        for hh in heads:
            s = s_buf[hh, s_slot]
            bias = bias_tile(hh, qi, t)
            if bias is not None:
                s = s + bias
            m_old = jnp.where(t == 0, NEG, m_ref[hh])
            m_new = jnp.maximum(m_old, jnp.max(s, axis=0, keepdims=True))
            m_ref[hh] = m_new
            p_buf[hh, p_slot] = jnp.exp2(s - m_new).astype(BF16)
            a_buf[hh, p_slot] = jnp.exp2(m_old - m_new)

    def pv_stage(qi, t, p_slot):
        for hh in heads:
            pv = jnp.dot(vt_tile(hh, t), p_buf[hh, p_slot], preferred_element_type=F32)
            qs = qi & (Q_SLOTS - 1)
            acc_ref[hh, qs] = a_buf[hh, p_slot] * acc_ref[hh, qs] + pv

    def finalize_all(qi):
        for hh in heads:
            finalize(hh, qi, acc_ref[hh, qi & (Q_SLOTS - 1)])

    m_ref[...] = jnp.full(m_ref.shape, NEG, F32)
    acc_ref[...] = jnp.zeros(acc_ref.shape, F32)
    p_buf[...] = jnp.zeros(p_buf.shape, BF16)
    a_buf[...] = jnp.zeros(a_buf.shape, F32)

    zero = jnp.int32(0)
    ahead = [(zero, zero)]
    for _ in range(QK_LEAD):
        ahead.append(advance(*ahead[-1]))
    for i in range(QK_LEAD):
        qk_stage(*ahead[i], i)

    def step(carry, j):
        lead, cur, prv = carry[:QK_LEAD], carry[QK_LEAD], carry[QK_LEAD + 1]
        pv_stage(*prv, (j + 1) % 2)
        qk_stage(*lead[0], (j + QK_LEAD) % S_SLOTS)
        sm_stage(*cur, j % S_SLOTS, j % 2)
        return (advance(*lead[0]),) + lead + (cur,), (prv[1] == prv[0], prv[0])

    reuse_distance = (Q_SLOTS - 1) * (Q_SLOTS + 2) // 2 + 1
    assert sum(n * steps for n, steps in trip_plan) == total
    assert all(steps % S_SLOTS == 0 and steps < reuse_distance for _, steps in trip_plan)

    def make_body(steps):
        def body(_, carry):
            done = []
            for j in range(steps):
                carry, d = step(carry, j)
                done.append(d)
            for ended, q in done:
                @pl.when(ended)
                def _():
                    finalize_all(q)
            return carry
        return body

    carry = tuple(reversed(ahead)) + ((zero, jnp.int32(1)),)
    for n_trips, steps in trip_plan:
        carry = lax.fori_loop(0, n_trips, make_body(steps), carry)
    pv_stage(*carry[-1], (total - 1) % 2)
    finalize_all(carry[-1][0])


def _attn_scratch(nh, n_kv, tk, n):
    return [pltpu.VMEM((nh, n_kv, DV_AUG, tk), BF16),
            pltpu.VMEM((nh, S_SLOTS, tk, n), F32),
            pltpu.VMEM((nh, 2, tk, n), BF16),
            pltpu.VMEM((nh, 2, 1, n), F32),
            pltpu.VMEM((nh, 1, n), F32),
            pltpu.VMEM((nh, Q_SLOTS, DV_AUG, n), F32)]


def _fill_vt(vt_buf, hh, t, v):
    dv, tk = v.shape[1], v.shape[0]
    vt_buf[hh, t, :dv, :] = v.T
    row = lax.broadcasted_iota(jnp.int32, (DV_AUG - dv, tk), 0)
    vt_buf[hh, t, dv:, :] = jnp.where(row == 0, 1.0, 0.0).astype(BF16)


def _diff_attn_kernel(lam_ref, q_ref, k_ref, v_ref, gate_ref, tab_ref, g_ref, o_ref,
                      qz_buf, vt_buf, *scratch, out_scale):
    tq = TQ_DIFF
    hd = 2 * DIFF_HEAD_DIM
    assert hd == MLA_V_DIM
    n_q = q_ref.shape[1] // tq
    heads = range(NH_DIFF)
    lam = lam_ref[0, 0]

    def rows_of(i):
        return pl.ds(pl.multiple_of(i * tq, tq), tq)

    def cols_of(hh):
        return slice(hh * hd, (hh + 1) * hd)

    feat = lax.broadcasted_iota(jnp.int32, (hd, tq), 0)

    def prep(qi, carry):
        for hh in heads:
            qb = q_ref[0, rows_of(qi), cols_of(hh)].T
            zero = jnp.zeros_like(qb)
            qz_buf[hh, qi, :, :tq] = jnp.where(feat < DIFF_HEAD_DIM, qb, zero)
            qz_buf[hh, qi, :, tq:] = jnp.where(feat >= DIFF_HEAD_DIM, qb, zero)
            _fill_vt(vt_buf, hh, qi, v_ref[0, rows_of(qi), cols_of(hh)])
        return carry
    lax.fori_loop(0, n_q, prep, 0)

    def qk_operands(hh, qi, t):
        return k_ref[0, rows_of(t), cols_of(hh)], qz_buf[hh, qi]

    def vt_tile(hh, t):
        return vt_buf[hh, t]

    def bias_tile(hh, qi, t):
        return tab_ref[hh, jnp.maximum(t - qi + 2, 0)]

    def finalize(hh, qi, acc):
        inv = 1.0 / acc[hd:hd + 1]
        o_t = acc[:hd, :tq] * inv[:, :tq] - lam * (acc[:hd, tq:] * inv[:, tq:])
        o_t = o_t * lax.rsqrt(jnp.mean(o_t * o_t, axis=0, keepdims=True) + EPS)
        o = o_t.T * g_ref[...] * out_scale
        gate = gate_ref[0, rows_of(qi), cols_of(hh)].astype(F32)
        o_ref[0, rows_of(qi), cols_of(hh)] = (o * gate).astype(BF16)

    _attn_pipeline(n_q, TRIPS_DIFF, heads, qk_operands, vt_tile, bias_tile, finalize, *scratch)


def _diff_attn(main3, lam, tab, g, lam_init):
    b, s, _ = main3.shape
    tq = TQ_DIFF
    hd = 2 * DIFF_HEAD_DIM
    nh = NH_DIFF
    blk = nh * hd
    kcol, vcol, gcol = DIFF_WIDTH // blk, 2 * DIFF_WIDTH // blk, QKV_WIDTH // blk
    seq_blk = lambda off: pl.BlockSpec((1, s, blk), lambda bi, h: (bi, 0, off + h))
    return pl.pallas_call(
        functools.partial(_diff_attn_kernel, out_scale=1.0 - lam_init),
        grid=(b, DIFF_HEADS // nh),
        in_specs=[
            pl.BlockSpec(memory_space=pltpu.SMEM),
            seq_blk(0), seq_blk(kcol), seq_blk(vcol), seq_blk(gcol),
            pl.BlockSpec((nh, 3, tq, 2 * tq), lambda bi, h: (h, 0, 0, 0)),
            pl.BlockSpec((1, hd), lambda bi, h: (0, 0)),
        ],
        out_specs=seq_blk(0),
        out_shape=jax.ShapeDtypeStruct((b, s, DIFF_WIDTH), BF16),
        scratch_shapes=[pltpu.VMEM((nh, s // tq, hd, 2 * tq), BF16)]
        + _attn_scratch(nh, s // tq, tq, 2 * tq),
        compiler_params=pltpu.CompilerParams(
            dimension_semantics=("parallel", "arbitrary"), vmem_limit_bytes=VMEM_LIMIT),
        name="diff_attn",
    )(lam, main3, main3, main3, main3, tab, g)


def _mla_attn_kernel(qn_ref, qr_ref, kv_ref, kr_ref, gate_ref, o_ref, qt_buf, vt_buf, *scratch):
    tq = TQ_MLA
    n_q = qn_ref.shape[1] // tq
    heads = range(NH_MLA)

    def rows_of(i):
        return pl.ds(pl.multiple_of(i * tq, tq), tq)

    def cols_of(hh):
        return slice(hh * LANE, (hh + 1) * LANE)

    def prep(qi, carry):
        for hh in heads:
            qt_buf[hh, qi, :LANE, :] = qn_ref[0, rows_of(qi), cols_of(hh)].T
            qt_buf[hh, qi, LANE:, :] = qr_ref[0, rows_of(qi), cols_of(hh)].T
            _fill_vt(vt_buf, hh, qi, kv_ref[0, rows_of(qi), cols_of(2 * hh + 1)])
        return carry
    lax.fori_loop(0, n_q, prep, 0)

    def qk_operands(hh, qi, t):
        kt = jnp.concatenate([kv_ref[0, rows_of(t), cols_of(2 * hh)], kr_ref[0, rows_of(t), :]],
                             axis=1)
        return kt, qt_buf[hh, qi]

    def vt_tile(hh, t):
        return vt_buf[hh, t]

    def finalize(hh, qi, acc):
        dv = MLA_V_DIM
        o = (acc[:dv] * (1.0 / acc[dv:dv + 1])).T
        gate = gate_ref[0, rows_of(qi), cols_of(hh)].astype(F32)
        o_ref[0, rows_of(qi), cols_of(hh)] = (o * gate).astype(BF16)

    _attn_pipeline(n_q, TRIPS_MLA, heads, qk_operands, vt_tile, lambda hh, qi, t: None, finalize,
                   *scratch)


def _mla_attn(qn3, qr3, kv3, kr3, main3):
    b, s, _ = qn3.shape
    tq = TQ_MLA
    nh = NH_MLA
    blk = nh * LANE
    gcol = (QKV_WIDTH + DIFF_WIDTH) // blk
    seq_blk = lambda w, off: pl.BlockSpec((1, s, w), lambda bi, h: (bi, 0, off + h))
    return pl.pallas_call(
        _mla_attn_kernel,
        grid=(b, MLA_HEADS // nh),
        in_specs=[
            seq_blk(blk, 0), seq_blk(blk, 0), seq_blk(2 * blk, 0),
            pl.BlockSpec((1, s, LANE), lambda bi, h: (bi, 0, 0)),
            seq_blk(blk, gcol),
        ],
        out_specs=seq_blk(blk, 0),
        out_shape=jax.ShapeDtypeStruct((b, s, MLA_WIDTH), BF16),
        scratch_shapes=[pltpu.VMEM((nh, s // tq, 2 * LANE, tq), BF16)]
        + _attn_scratch(nh, s // tq, tq, tq),
        compiler_params=pltpu.CompilerParams(
            dimension_semantics=("parallel", "arbitrary"), vmem_limit_bytes=VMEM_LIMIT),
        name="mla_attn",
    )(qn3, qr3, kv3, kr3, main3)


def _out_proj_kernel(x_ref, ya_ref, yb_ref, w_ref, gf_ref, o_ref, *, final):
    acc = jnp.dot(ya_ref[...], w_ref[:DIFF_WIDTH, :], preferred_element_type=F32)
    acc = acc + jnp.dot(yb_ref[...], w_ref[DIFF_WIDTH:, :], preferred_element_type=F32)
    xn = x_ref[...] + acc
    if final:
        xn = xn * _rms_scale(xn) * gf_ref[...]
    o_ref[...] = xn


def _out_proj(x2, ya, yb, w_out, gf, final):
    m = x2.shape[0]
    row = lambda i: (i, 0)
    const = lambda i: (0, 0)
    return pl.pallas_call(
        functools.partial(_out_proj_kernel, final=final),
        grid=(m // TM_OUT,),
        in_specs=[
            pl.BlockSpec((TM_OUT, D_MODEL), row),
            pl.BlockSpec((TM_OUT, DIFF_WIDTH), row),
            pl.BlockSpec((TM_OUT, MLA_WIDTH), row),
            pl.BlockSpec((MIX_WIDTH, D_MODEL), const),
            pl.BlockSpec((1, D_MODEL), const),
        ],
        out_specs=pl.BlockSpec((TM_OUT, D_MODEL), row),
        out_shape=jax.ShapeDtypeStruct((m, D_MODEL), F32),
        compiler_params=pltpu.CompilerParams(
            dimension_semantics=("parallel",), vmem_limit_bytes=VMEM_LIMIT),
        name="out_proj",
    )(x2, ya, yb, w_out, gf)


def _rel_bucket(rel):
    nb = REL_BUCKETS // 2
    max_exact = nb // 2
    ret = (rel > 0).astype(jnp.int32) * nb
    n = jnp.abs(rel)
    nf = jnp.maximum(n, 1).astype(F32)
    large = max_exact + (jnp.log(nf / max_exact) / math.log(REL_MAX_DIST / max_exact)
                         * (nb - max_exact)).astype(jnp.int32)
    large = jnp.minimum(large, nb - 1)
    return ret + jnp.where(n < max_exact, n, large)


def _bias_tables(rel_bias):
    tq = TQ_DIFF
    kk = jnp.arange(2 * tq, dtype=jnp.int32)[:, None]
    qq = jnp.arange(tq, dtype=jnp.int32)[None, :]
    onehot = (_rel_bucket(kk - tq - qq)[..., None] == jnp.arange(REL_BUCKETS)).astype(F32)
    bias = jnp.einsum('kqb,bc->kqc', onehot, rel_bias.astype(F32),
                      precision=lax.Precision.HIGHEST) * LOG2E
    allowed = (kk < tq) | (((kk - tq) // CHUNK) <= (qq // CHUNK))
    bias = jnp.where(allowed[..., None], bias, NEG)
    near = bias.reshape(2, tq, tq, DIFF_HEADS, 2).transpose(3, 0, 1, 4, 2)
    near = near.reshape(DIFF_HEADS, 2, tq, 2 * tq)
    far = rel_bias[_rel_bucket(jnp.asarray(-2 * REL_MAX_DIST, jnp.int32))].astype(F32) * LOG2E
    far = jnp.repeat(far.reshape(DIFF_HEADS, 2), tq, axis=1)
    far = jnp.broadcast_to(far[:, None, None, :], (DIFF_HEADS, 1, tq, 2 * tq))
    return jnp.concatenate([far, near], axis=1)


def _pos_tables(seq):
    pos = jnp.arange(seq, dtype=F32)
    inv_freq = ROPE_BASE ** (-jnp.arange(0, MLA_ROPE, 2, dtype=F32) / MLA_ROPE)
    ang = pos[:, None] * inv_freq[None, :]
    cos, sin = jnp.cos(ang), jnp.sin(ang)
    pad = jnp.zeros((seq, LANE - MLA_ROPE), F32)
    n_chunks = seq // CHUNK
    assert n_chunks <= LANE - MLA_ROPE
    chunk = (jnp.arange(seq, dtype=jnp.int32) // CHUNK)[:, None]
    cid = jnp.arange(LANE, dtype=jnp.int32)[None, :] - MLA_ROPE
    valid = (cid >= 0) & (cid < n_chunks)
    mq = jnp.where(valid & (cid > chunk), NEG, 0.0).astype(F32)
    mk = jnp.where(valid & (cid == chunk), 1.0, 0.0).astype(F32)
    return (jnp.concatenate([cos, cos, pad], axis=1),
            jnp.concatenate([-sin, sin, pad], axis=1), mq, mk)


def _swap_halves(w):
    half = w.shape[-1] // 2
    return jnp.concatenate([w[..., half:], w[..., :half]], axis=-1)


def _pad_lanes(w):
    return jnp.pad(w, [(0, 0)] * (w.ndim - 1) + [(0, LANE - w.shape[-1])])


def kernel(x, norm_g, w_in, diff_lambda, diff_subln_g, mla_q_norm_g, mla_w_q_b,
           mla_kv_norm_g, mla_w_kv_b, w_out, rel_bias, final_norm_g):
    b, s, d = x.shape
    assert d == D_MODEL and s % TQ_MLA == 0 and s >= 2 * TQ_MLA and (b * s) % TM_PROJ == 0
    m = b * s
    pos_tables = _pos_tables(s)
    tab = _bias_tables(rel_bias)
    colscale = jnp.concatenate([
        jnp.full((1, DIFF_WIDTH), DIFF_HEAD_DIM ** -0.5 * LOG2E, F32),
        jnp.ones((1, MAIN_WIDTH - DIFF_WIDTH), F32)], axis=1)
    gf = final_norm_g.reshape(1, D_MODEL)

    x2 = x.reshape(m, d)
    for l in range(DEPTH):
        wl = w_in[l]
        w_main = jnp.concatenate([wl[:, :QKV_WIDTH], wl[:, GATE_OFF:]], axis=1).astype(BF16)
        w_kr = wl[:, GATE_OFF - MLA_ROPE:GATE_OFF]
        w_lat = jnp.concatenate([wl[:, LAT_OFF:GATE_OFF - MLA_ROPE], _pad_lanes(w_kr),
                                 _pad_lanes(_swap_halves(w_kr))], axis=1).astype(BF16)
        wq = mla_w_q_b[l].reshape(MLA_Q_LORA, MLA_HEADS, MLA_NOPE + MLA_ROPE)
        wq_rope = wq[:, :, MLA_NOPE:]
        w_q2 = jnp.concatenate([
            wq[:, :, :MLA_NOPE].reshape(MLA_Q_LORA, -1),
            _pad_lanes(wq_rope).reshape(MLA_Q_LORA, -1),
            _pad_lanes(_swap_halves(wq_rope)).reshape(MLA_Q_LORA, -1)], axis=1).astype(BF16)
        w_kv2 = mla_w_kv_b[l].astype(BF16)
        g = norm_g[l].reshape(1, D_MODEL)

        lam_init = 0.8 - 0.6 * math.exp(-0.3 * l)
        lp = diff_lambda[l].astype(F32)
        lam = jnp.exp(jnp.sum(lp[0] * lp[1])) - jnp.exp(jnp.sum(lp[2] * lp[3])) + lam_init

        main = _in_proj(x2, g, w_main, colscale)
        qn, qr, kv, kr = _latent(x2, g, w_lat, mla_q_norm_g[l].reshape(1, -1),
                                 mla_kv_norm_g[l].reshape(1, -1), w_q2, w_kv2, pos_tables, s)
        main3 = main.reshape(b, s, MAIN_WIDTH)
        ya = _diff_attn(main3, lam.reshape(1, 1), tab, diff_subln_g[l].reshape(1, -1), lam_init)
        yb = _mla_attn(qn.reshape(b, s, -1), qr.reshape(b, s, -1), kv.reshape(b, s, -1),
                       kr.reshape(b, s, -1), main3)
        x2 = _out_proj(x2, ya.reshape(m, -1), yb.reshape(m, -1), w_out[l].astype(BF16), gf,
                       final=(l == DEPTH - 1))
    return x2.reshape(b, s, d)
```

```python
import functools
import math

import jax
import jax.numpy as jnp
from jax import lax
from jax.experimental import pallas as pl
from jax.experimental.pallas import tpu as pltpu

D_MODEL = 2048
DEPTH = 2
CHUNK = 64
MIX_WIDTH = D_MODEL
DIFF_WIDTH = MIX_WIDTH // 2
MLA_WIDTH = MIX_WIDTH - DIFF_WIDTH
DIFF_HEAD_DIM = 64
DIFF_HEADS = DIFF_WIDTH // (2 * DIFF_HEAD_DIM)
MLA_V_DIM = 128
MLA_HEADS = MLA_WIDTH // MLA_V_DIM
MLA_NOPE = 128
MLA_ROPE = 64
MLA_Q_LORA = 512
MLA_KV_LORA = 256
ROPE_BASE = 10000.0
REL_BUCKETS = 32
REL_MAX_DIST = 128
EPS = 1e-6
NEG = -1e30
QKV_WIDTH = 3 * DIFF_WIDTH
LAT_OFF = QKV_WIDTH
GATE_OFF = LAT_OFF + MLA_Q_LORA + MLA_KV_LORA + MLA_ROPE
MAIN_WIDTH = QKV_WIDTH + MIX_WIDTH

LOG2E = math.log2(math.e)
LANE = 128
VMEM_LIMIT = 56 * 1024 * 1024

BF16 = jnp.bfloat16
F32 = jnp.float32

TM_PROJ = 1024
TN_PROJ = 1024
RC_NORM = 256
TM_LAT = 512
TM_OUT = 512
TQ_DIFF = 256
TQ_MLA = 512
NH_DIFF = 1
NH_MLA = 1
TRIPS_DIFF = ((11, 12), (1, 4))
TRIPS_MLA = ((3, 12),)
Q_SLOTS = 8
QK_LEAD = 2
S_SLOTS = 4

DV_AUG = MLA_V_DIM + 16


def _rms_scale(xf):
    return lax.rsqrt(jnp.mean(xf * xf, axis=-1, keepdims=True) + EPS)


def _in_proj_kernel(x_ref, g_ref, w_ref, cs_ref, o_ref, h_ref, *, n_plain):
    j = pl.program_id(1)

    @pl.when(j == 0)
    def _():
        def body(c, carry):
            rows = pl.ds(pl.multiple_of(c * RC_NORM, RC_NORM), RC_NORM)
            xf = x_ref[rows, :]
            h_ref[rows, :] = (xf * _rms_scale(xf) * g_ref[...]).astype(BF16)
            return carry
        lax.fori_loop(0, TM_PROJ // RC_NORM, body, 0)

    acc = jnp.dot(h_ref[...], w_ref[...], preferred_element_type=F32) * cs_ref[...]

    @pl.when(j < n_plain)
    def _():
        o_ref[...] = acc.astype(BF16)

    @pl.when(j >= n_plain)
    def _():
        o_ref[...] = (acc * jax.nn.sigmoid(acc)).astype(BF16)


def _in_proj(x2, g, w_main, colscale):
    m = x2.shape[0]
    n = w_main.shape[1]
    return pl.pallas_call(
        functools.partial(_in_proj_kernel, n_plain=QKV_WIDTH // TN_PROJ),
        grid=(m // TM_PROJ, n // TN_PROJ),
        in_specs=[
            pl.BlockSpec((TM_PROJ, D_MODEL), lambda i, j: (i, 0)),
            pl.BlockSpec((1, D_MODEL), lambda i, j: (0, 0)),
            pl.BlockSpec((D_MODEL, TN_PROJ), lambda i, j: (0, j)),
            pl.BlockSpec((1, TN_PROJ), lambda i, j: (0, j)),
        ],
        out_specs=pl.BlockSpec((TM_PROJ, TN_PROJ), lambda i, j: (i, j)),
        out_shape=jax.ShapeDtypeStruct((m, n), BF16),
        scratch_shapes=[pltpu.VMEM((TM_PROJ, D_MODEL), BF16)],
        compiler_params=pltpu.CompilerParams(
            dimension_semantics=("parallel", "arbitrary"), vmem_limit_bytes=VMEM_LIMIT),
        name="in_proj",
    )(x2, g, w_main, colscale)


def _latent_kernel(x_ref, g_ref, wl_ref, gq_ref, gkv_ref, wq_ref, wkv_ref, cos_ref, sin_ref,
                   mq_ref, mk_ref, qn_ref, qr_ref, kv_ref, kr_ref, *, q_scale):
    xf = x_ref[...]
    h = (xf * _rms_scale(xf) * g_ref[...]).astype(BF16)
    lat = jnp.dot(h, wl_ref[...], preferred_element_type=F32)
    cq = lat[:, :MLA_Q_LORA]
    ckv = lat[:, MLA_Q_LORA:MLA_Q_LORA + MLA_KV_LORA]
    ka = lat[:, MLA_Q_LORA + MLA_KV_LORA:MLA_Q_LORA + MLA_KV_LORA + LANE]
    kb = lat[:, MLA_Q_LORA + MLA_KV_LORA + LANE:]
    cos = cos_ref[...]
    sin = sin_ref[...]
    mq = mq_ref[...]

    cqn = (cq * _rms_scale(cq) * gq_ref[...]).astype(BF16)
    qall = jnp.dot(cqn, wq_ref[...], preferred_element_type=F32) * q_scale
    nope_w = MLA_HEADS * MLA_NOPE
    qn_ref[...] = qall[:, :nope_w].astype(BF16)
    for hh in range(MLA_HEADS):
        a = qall[:, nope_w + hh * LANE:nope_w + (hh + 1) * LANE]
        b = qall[:, 2 * nope_w + hh * LANE:2 * nope_w + (hh + 1) * LANE]
        qr_ref[:, hh * LANE:(hh + 1) * LANE] = (a * cos + b * sin + mq).astype(BF16)

    ckvn = (ckv * _rms_scale(ckv) * gkv_ref[...]).astype(BF16)
    kv_ref[...] = jnp.dot(ckvn, wkv_ref[...], preferred_element_type=F32).astype(BF16)
    kr_ref[...] = (ka * cos + kb * sin + mk_ref[...]).astype(BF16)


def _latent(x2, g, w_lat, gq, gkv, w_q2, w_kv2, pos_tables, seq):
    m = x2.shape[0]
    seq_tiles = seq // TM_LAT
    const = lambda i: (0, 0)
    row = lambda i: (i, 0)
    pos = lambda i: (i % seq_tiles, 0)
    q_scale = (MLA_NOPE + MLA_ROPE) ** -0.5 * LOG2E
    return pl.pallas_call(
        functools.partial(_latent_kernel, q_scale=q_scale),
        grid=(m // TM_LAT,),
        in_specs=[
            pl.BlockSpec((TM_LAT, D_MODEL), row),
            pl.BlockSpec((1, D_MODEL), const),
            pl.BlockSpec(w_lat.shape, const),
            pl.BlockSpec((1, MLA_Q_LORA), const),
            pl.BlockSpec((1, MLA_KV_LORA), const),
            pl.BlockSpec(w_q2.shape, const),
            pl.BlockSpec(w_kv2.shape, const),
        ] + [pl.BlockSpec((TM_LAT, LANE), pos)] * len(pos_tables),
        out_specs=[
            pl.BlockSpec((TM_LAT, MLA_HEADS * LANE), row),
            pl.BlockSpec((TM_LAT, MLA_HEADS * LANE), row),
            pl.BlockSpec((TM_LAT, w_kv2.shape[1]), row),
            pl.BlockSpec((TM_LAT, LANE), row),
        ],
        out_shape=[
            jax.ShapeDtypeStruct((m, MLA_HEADS * LANE), BF16),
            jax.ShapeDtypeStruct((m, MLA_HEADS * LANE), BF16),
            jax.ShapeDtypeStruct((m, w_kv2.shape[1]), BF16),
            jax.ShapeDtypeStruct((m, LANE), BF16),
        ],
        compiler_params=pltpu.CompilerParams(
            dimension_semantics=("parallel",), vmem_limit_bytes=VMEM_LIMIT),
        name="latent_proj",
    )(x2, g, w_lat, gq, gkv, w_q2, w_kv2, *pos_tables)


def _attn_pipeline(n_q, trip_plan, heads, qk_operands, vt_tile, bias_tile, finalize,
                   s_buf, mt_buf, p_buf, a_buf, m_ref, acc_ref):
    last = n_q - 1
    total = n_q * (n_q + 1) // 2

    def advance(q, t):
        end = t >= q
        return (jnp.where(end, jnp.minimum(q + 1, last), q),
                jnp.where(end, jnp.where(q >= last, t, 0), t + 1))

    def qk_stage(qi, t, slot):
        for hh in heads:
            kt, qt = qk_operands(hh, qi, t)
            s = jnp.dot(kt, qt, preferred_element_type=F32)
            bias = bias_tile(hh, qi, t)
            if bias is not None:
                s = s + bias
            s_buf[hh, slot] = s
            mt_buf[hh, slot] = jnp.max(s, axis=0, keepdims=True)

    def sm_stage(qi, t, s_slot, p_slot):
        for hh in heads:
            s = s_buf[hh, s_slot]
            m_old = jnp.where(t == 0, NEG, m_ref[hh])
            m_new = jnp.maximum(m_old, mt_buf[hh, s_slot])
            m_ref[hh] = m_new
            p_buf[hh, p_slot] = jnp.exp2(s - m_new).astype(BF16)
            a_buf[hh, p_slot] = jnp.exp2(m_old - m_new)

    def pv_stage(qi, t, p_slot):
        for hh in heads:
            pv = jnp.dot(vt_tile(hh, t), p_buf[hh, p_slot], preferred_element_type=F32)
            qs = qi & (Q_SLOTS - 1)
            acc_ref[hh, qs] = a_buf[hh, p_slot] * acc_ref[hh, qs] + pv

    def finalize_all(qi):
        for hh in heads:
            finalize(hh, qi, acc_ref[hh, qi & (Q_SLOTS - 1)])

    m_ref[...] = jnp.full(m_ref.shape, NEG, F32)
    acc_ref[...] = jnp.zeros(acc_ref.shape, F32)
    p_buf[...] = jnp.zeros(p_buf.shape, BF16)
    a_buf[...] = jnp.zeros(a_buf.shape, F32)

    zero = jnp.int32(0)
    ahead = [(zero, zero)]
    for _ in range(QK_LEAD):
        ahead.append(advance(*ahead[-1]))
    for i in range(QK_LEAD):
        qk_stage(*ahead[i], i)

    def step(carry, j):
        lead, cur, prv = carry[:QK_LEAD], carry[QK_LEAD], carry[QK_LEAD + 1]
        pv_stage(*prv, (j + 1) % 2)
        qk_stage(*lead[0], (j + QK_LEAD) % S_SLOTS)
        sm_stage(*cur, j % S_SLOTS, j % 2)
        return (advance(*lead[0]),) + lead + (cur,), (prv[1] == prv[0], prv[0])

    reuse_distance = (Q_SLOTS - 1) * (Q_SLOTS + 2) // 2 + 1
    assert sum(n * steps for n, steps in trip_plan) == total
    assert all(steps % S_SLOTS == 0 and steps < reuse_distance for _, steps in trip_plan)

    def make_body(steps):
        def body(_, carry):
            done = []
            for j in range(steps):
                carry, d = step(carry, j)
                done.append(d)
            for ended, q in done:
                @pl.when(ended)
                def _():
                    finalize_all(q)
            return carry
        return body

    carry = tuple(reversed(ahead)) + ((zero, jnp.int32(1)),)
    for n_trips, steps in trip_plan:
        carry = lax.fori_loop(0, n_trips, make_body(steps), carry)
    pv_stage(*carry[-1], (total - 1) % 2)
    finalize_all(carry[-1][0])


def _attn_scratch(nh, n_kv, tk, n):
    return [pltpu.VMEM((nh, n_kv, DV_AUG, tk), BF16),
            pltpu.VMEM((nh, S_SLOTS, tk, n), F32),
            pltpu.VMEM((nh, S_SLOTS, 1, n), F32),
            pltpu.VMEM((nh, 2, tk, n), BF16),
            pltpu.VMEM((nh, 2, 1, n), F32),
            pltpu.VMEM((nh, 1, n), F32),
            pltpu.VMEM((nh, Q_SLOTS, DV_AUG, n), F32)]


def _fill_vt(vt_buf, hh, t, v):
    dv, tk = v.shape[1], v.shape[0]
    vt_buf[hh, t, :dv, :] = v.T
    row = lax.broadcasted_iota(jnp.int32, (DV_AUG - dv, tk), 0)
    vt_buf[hh, t, dv:, :] = jnp.where(row == 0, 1.0, 0.0).astype(BF16)


def _diff_attn_kernel(lam_ref, q_ref, k_ref, v_ref, gate_ref, tab_ref, g_ref, o_ref,
                      qz_buf, vt_buf, *scratch, out_scale):
    tq = TQ_DIFF
    hd = 2 * DIFF_HEAD_DIM
    assert hd == MLA_V_DIM
    n_q = q_ref.shape[1] // tq
    heads = range(NH_DIFF)
    lam = lam_ref[0, 0]

    def rows_of(i):
        return pl.ds(pl.multiple_of(i * tq, tq), tq)

    def cols_of(hh):
        return slice(hh * hd, (hh + 1) * hd)

    feat = lax.broadcasted_iota(jnp.int32, (hd, tq), 0)

    def prep(qi, carry):
        for hh in heads:
            qb = q_ref[0, rows_of(qi), cols_of(hh)].T
            zero = jnp.zeros_like(qb)
            qz_buf[hh, qi, :, :tq] = jnp.where(feat < DIFF_HEAD_DIM, qb, zero)
            qz_buf[hh, qi, :, tq:] = jnp.where(feat >= DIFF_HEAD_DIM, qb, zero)
            _fill_vt(vt_buf, hh, qi, v_ref[0, rows_of(qi), cols_of(hh)])
        return carry
    lax.fori_loop(0, n_q, prep, 0)

    def qk_operands(hh, qi, t):
        return k_ref[0, rows_of(t), cols_of(hh)], qz_buf[hh, qi]

    def vt_tile(hh, t):
        return vt_buf[hh, t]

    def bias_tile(hh, qi, t):
        return tab_ref[hh, jnp.maximum(t - qi + 2, 0)]

    def finalize(hh, qi, acc):
        inv = 1.0 / acc[hd:hd + 1]
        o_t = acc[:hd, :tq] * inv[:, :tq] - lam * (acc[:hd, tq:] * inv[:, tq:])
        o_t = o_t * lax.rsqrt(jnp.mean(o_t * o_t, axis=0, keepdims=True) + EPS)
        o = o_t.T * g_ref[...] * out_scale
        gate = gate_ref[0, rows_of(qi), cols_of(hh)].astype(F32)
        o_ref[0, rows_of(qi), cols_of(hh)] = (o * gate).astype(BF16)

    _attn_pipeline(n_q, TRIPS_DIFF, heads, qk_operands, vt_tile, bias_tile, finalize, *scratch)


def _diff_attn(main3, lam, tab, g, lam_init):
    b, s, _ = main3.shape
    tq = TQ_DIFF
    hd = 2 * DIFF_HEAD_DIM
    nh = NH_DIFF
    blk = nh * hd
    kcol, vcol, gcol = DIFF_WIDTH // blk, 2 * DIFF_WIDTH // blk, QKV_WIDTH // blk
    seq_blk = lambda off: pl.BlockSpec((1, s, blk), lambda bi, h: (bi, 0, off + h))
    return pl.pallas_call(
        functools.partial(_diff_attn_kernel, out_scale=1.0 - lam_init),
        grid=(b, DIFF_HEADS // nh),
        in_specs=[
            pl.BlockSpec(memory_space=pltpu.SMEM),
            seq_blk(0), seq_blk(kcol), seq_blk(vcol), seq_blk(gcol),
            pl.BlockSpec((nh, 3, tq, 2 * tq), lambda bi, h: (h, 0, 0, 0)),
            pl.BlockSpec((1, hd), lambda bi, h: (0, 0)),
        ],
        out_specs=seq_blk(0),
        out_shape=jax.ShapeDtypeStruct((b, s, DIFF_WIDTH), BF16),
        scratch_shapes=[pltpu.VMEM((nh, s // tq, hd, 2 * tq), BF16)]
        + _attn_scratch(nh, s // tq, tq, 2 * tq),
        compiler_params=pltpu.CompilerParams(
            dimension_semantics=("parallel", "arbitrary"), vmem_limit_bytes=VMEM_LIMIT),
        name="diff_attn",
    )(lam, main3, main3, main3, main3, tab, g)


def _mla_attn_kernel(qn_ref, qr_ref, kv_ref, kr_ref, gate_ref, o_ref, qt_buf, vt_buf, *scratch):
    tq = TQ_MLA
    n_q = qn_ref.shape[1] // tq
    heads = range(NH_MLA)

    def rows_of(i):
        return pl.ds(pl.multiple_of(i * tq, tq), tq)

    def cols_of(hh):
        return slice(hh * LANE, (hh + 1) * LANE)

    def prep(qi, carry):
        for hh in heads:
            qt_buf[hh, qi, :LANE, :] = qn_ref[0, rows_of(qi), cols_of(hh)].T
            qt_buf[hh, qi, LANE:, :] = qr_ref[0, rows_of(qi), cols_of(hh)].T
            _fill_vt(vt_buf, hh, qi, kv_ref[0, rows_of(qi), cols_of(2 * hh + 1)])
        return carry
    lax.fori_loop(0, n_q, prep, 0)

    def qk_operands(hh, qi, t):
        kt = jnp.concatenate([kv_ref[0, rows_of(t), cols_of(2 * hh)], kr_ref[0, rows_of(t), :]],
                             axis=1)
        return kt, qt_buf[hh, qi]

    def vt_tile(hh, t):
        return vt_buf[hh, t]

    def finalize(hh, qi, acc):
        dv = MLA_V_DIM
        o = (acc[:dv] * (1.0 / acc[dv:dv + 1])).T
        gate = gate_ref[0, rows_of(qi), cols_of(hh)].astype(F32)
        o_ref[0, rows_of(qi), cols_of(hh)] = (o * gate).astype(BF16)

    _attn_pipeline(n_q, TRIPS_MLA, heads, qk_operands, vt_tile, lambda hh, qi, t: None, finalize,
                   *scratch)


def _mla_attn(qn3, qr3, kv3, kr3, main3):
    b, s, _ = qn3.shape
    tq = TQ_MLA
    nh = NH_MLA
    blk = nh * LANE
    gcol = (QKV_WIDTH + DIFF_WIDTH) // blk
    seq_blk = lambda w, off: pl.BlockSpec((1, s, w), lambda bi, h: (bi, 0, off + h))
    return pl.pallas_call(
        _mla_attn_kernel,
        grid=(b, MLA_HEADS // nh),
        in_specs=[
            seq_blk(blk, 0), seq_blk(blk, 0), seq_blk(2 * blk, 0),
            pl.BlockSpec((1, s, LANE), lambda bi, h: (bi, 0, 0)),
            seq_blk(blk, gcol),
        ],
        out_specs=seq_blk(blk, 0),
        out_shape=jax.ShapeDtypeStruct((b, s, MLA_WIDTH), BF16),
        scratch_shapes=[pltpu.VMEM((nh, s // tq, 2 * LANE, tq), BF16)]
        + _attn_scratch(nh, s // tq, tq, tq),
        compiler_params=pltpu.CompilerParams(
            dimension_semantics=("parallel", "arbitrary"), vmem_limit_bytes=VMEM_LIMIT),
        name="mla_attn",
    )(qn3, qr3, kv3, kr3, main3)


def _out_proj_kernel(x_ref, ya_ref, yb_ref, w_ref, gf_ref, o_ref, *, final):
    acc = jnp.dot(ya_ref[...], w_ref[:DIFF_WIDTH, :], preferred_element_type=F32)
    acc = acc + jnp.dot(yb_ref[...], w_ref[DIFF_WIDTH:, :], preferred_element_type=F32)
    xn = x_ref[...] + acc
    if final:
        xn = xn * _rms_scale(xn) * gf_ref[...]
    o_ref[...] = xn


def _out_proj(x2, ya, yb, w_out, gf, final):
    m = x2.shape[0]
    row = lambda i: (i, 0)
    const = lambda i: (0, 0)
    return pl.pallas_call(
        functools.partial(_out_proj_kernel, final=final),
        grid=(m // TM_OUT,),
        in_specs=[
            pl.BlockSpec((TM_OUT, D_MODEL), row),
            pl.BlockSpec((TM_OUT, DIFF_WIDTH), row),
            pl.BlockSpec((TM_OUT, MLA_WIDTH), row),
            pl.BlockSpec((MIX_WIDTH, D_MODEL), const),
            pl.BlockSpec((1, D_MODEL), const),
        ],
        out_specs=pl.BlockSpec((TM_OUT, D_MODEL), row),
        out_shape=jax.ShapeDtypeStruct((m, D_MODEL), F32),
        compiler_params=pltpu.CompilerParams(
            dimension_semantics=("parallel",), vmem_limit_bytes=VMEM_LIMIT),
        name="out_proj",
    )(x2, ya, yb, w_out, gf)


def _rel_bucket(rel):
    nb = REL_BUCKETS // 2
    max_exact = nb // 2
    ret = (rel > 0).astype(jnp.int32) * nb
    n = jnp.abs(rel)
    nf = jnp.maximum(n, 1).astype(F32)
    large = max_exact + (jnp.log(nf / max_exact) / math.log(REL_MAX_DIST / max_exact)
                         * (nb - max_exact)).astype(jnp.int32)
    large = jnp.minimum(large, nb - 1)
    return ret + jnp.where(n < max_exact, n, large)


def _bias_tables(rel_bias):
    tq = TQ_DIFF
    kk = jnp.arange(2 * tq, dtype=jnp.int32)[:, None]
    qq = jnp.arange(tq, dtype=jnp.int32)[None, :]
    onehot = (_rel_bucket(kk - tq - qq)[..., None] == jnp.arange(REL_BUCKETS)).astype(F32)
    bias = jnp.einsum('kqb,bc->kqc', onehot, rel_bias.astype(F32),
                      precision=lax.Precision.HIGHEST) * LOG2E
    allowed = (kk < tq) | (((kk - tq) // CHUNK) <= (qq // CHUNK))
    bias = jnp.where(allowed[..., None], bias, NEG)
    near = bias.reshape(2, tq, tq, DIFF_HEADS, 2).transpose(3, 0, 1, 4, 2)
    near = near.reshape(DIFF_HEADS, 2, tq, 2 * tq)
    far = rel_bias[_rel_bucket(jnp.asarray(-2 * REL_MAX_DIST, jnp.int32))].astype(F32) * LOG2E
    far = jnp.repeat(far.reshape(DIFF_HEADS, 2), tq, axis=1)
    far = jnp.broadcast_to(far[:, None, None, :], (DIFF_HEADS, 1, tq, 2 * tq))
    return jnp.concatenate([far, near], axis=1)


def _pos_tables(seq):
    pos = jnp.arange(seq, dtype=F32)
    inv_freq = ROPE_BASE ** (-jnp.arange(0, MLA_ROPE, 2, dtype=F32) / MLA_ROPE)
    ang = pos[:, None] * inv_freq[None, :]
    cos, sin = jnp.cos(ang), jnp.sin(ang)
    pad = jnp.zeros((seq, LANE - MLA_ROPE), F32)
    n_chunks = seq // CHUNK
    assert n_chunks <= LANE - MLA_ROPE
    chunk = (jnp.arange(seq, dtype=jnp.int32) // CHUNK)[:, None]
    cid = jnp.arange(LANE, dtype=jnp.int32)[None, :] - MLA_ROPE
    valid = (cid >= 0) & (cid < n_chunks)
    mq = jnp.where(valid & (cid > chunk), NEG, 0.0).astype(F32)
    mk = jnp.where(valid & (cid == chunk), 1.0, 0.0).astype(F32)
    return (jnp.concatenate([cos, cos, pad], axis=1),
            jnp.concatenate([-sin, sin, pad], axis=1), mq, mk)


def _swap_halves(w):
    half = w.shape[-1] // 2
    return jnp.concatenate([w[..., half:], w[..., :half]], axis=-1)


def _pad_lanes(w):
    return jnp.pad(w, [(0, 0)] * (w.ndim - 1) + [(0, LANE - w.shape[-1])])


def kernel(x, norm_g, w_in, diff_lambda, diff_subln_g, mla_q_norm_g, mla_w_q_b,
           mla_kv_norm_g, mla_w_kv_b, w_out, rel_bias, final_norm_g):
    b, s, d = x.shape
    assert d == D_MODEL and s % TQ_MLA == 0 and s >= 2 * TQ_MLA and (b * s) % TM_PROJ == 0
    m = b * s
    pos_tables = _pos_tables(s)
    tab = _bias_tables(rel_bias)
    colscale = jnp.concatenate([
        jnp.full((1, DIFF_WIDTH), DIFF_HEAD_DIM ** -0.5 * LOG2E, F32),
        jnp.ones((1, MAIN_WIDTH - DIFF_WIDTH), F32)], axis=1)
    gf = final_norm_g.reshape(1, D_MODEL)

    x2 = x.reshape(m, d)
    for l in range(DEPTH):
        wl = w_in[l]
        w_main = jnp.concatenate([wl[:, :QKV_WIDTH], wl[:, GATE_OFF:]], axis=1).astype(BF16)
        w_kr = wl[:, GATE_OFF - MLA_ROPE:GATE_OFF]
        w_lat = jnp.concatenate([wl[:, LAT_OFF:GATE_OFF - MLA_ROPE], _pad_lanes(w_kr),
                                 _pad_lanes(_swap_halves(w_kr))], axis=1).astype(BF16)
        wq = mla_w_q_b[l].reshape(MLA_Q_LORA, MLA_HEADS, MLA_NOPE + MLA_ROPE)
        wq_rope = wq[:, :, MLA_NOPE:]
        w_q2 = jnp.concatenate([
            wq[:, :, :MLA_NOPE].reshape(MLA_Q_LORA, -1),
            _pad_lanes(wq_rope).reshape(MLA_Q_LORA, -1),
            _pad_lanes(_swap_halves(wq_rope)).reshape(MLA_Q_LORA, -1)], axis=1).astype(BF16)
        w_kv2 = mla_w_kv_b[l].astype(BF16)
        g = norm_g[l].reshape(1, D_MODEL)

        lam_init = 0.8 - 0.6 * math.exp(-0.3 * l)
        lp = diff_lambda[l].astype(F32)
        lam = jnp.exp(jnp.sum(lp[0] * lp[1])) - jnp.exp(jnp.sum(lp[2] * lp[3])) + lam_init

        main = _in_proj(x2, g, w_main, colscale)
        qn, qr, kv, kr = _latent(x2, g, w_lat, mla_q_norm_g[l].reshape(1, -1),
                                 mla_kv_norm_g[l].reshape(1, -1), w_q2, w_kv2, pos_tables, s)
        main3 = main.reshape(b, s, MAIN_WIDTH)
        ya = _diff_attn(main3, lam.reshape(1, 1), tab, diff_subln_g[l].reshape(1, -1), lam_init)
        yb = _mla_attn(qn.reshape(b, s, -1), qr.reshape(b, s, -1), kv.reshape(b, s, -1),
                       kr.reshape(b, s, -1), main3)
        x2 = _out_proj(x2, ya.reshape(m, -1), yb.reshape(m, -1), w_out[l].astype(BF16), gf,
                       final=(l == DEPTH - 1))
    return x2.reshape(b, s, d)
```

```python
import functools
import math

import jax
import jax.numpy as jnp
from jax import lax
from jax.experimental import pallas as pl
from jax.experimental.pallas import tpu as pltpu

D_MODEL = 2048
DEPTH = 2
CHUNK = 64
MIX_WIDTH = D_MODEL
DIFF_WIDTH = MIX_WIDTH // 2
MLA_WIDTH = MIX_WIDTH - DIFF_WIDTH
DIFF_HEAD_DIM = 64
DIFF_HEADS = DIFF_WIDTH // (2 * DIFF_HEAD_DIM)
MLA_V_DIM = 128
MLA_HEADS = MLA_WIDTH // MLA_V_DIM
MLA_NOPE = 128
MLA_ROPE = 64
MLA_Q_LORA = 512
MLA_KV_LORA = 256
ROPE_BASE = 10000.0
REL_BUCKETS = 32
REL_MAX_DIST = 128
EPS = 1e-6
NEG = -1e30
QKV_WIDTH = 3 * DIFF_WIDTH
LAT_OFF = QKV_WIDTH
GATE_OFF = LAT_OFF + MLA_Q_LORA + MLA_KV_LORA + MLA_ROPE

LOG2E = math.log2(math.e)
LANE = 128
VMEM_LIMIT = 56 * 1024 * 1024

BF16 = jnp.bfloat16
F32 = jnp.float32

TM_PROJ = 1024
TN_PROJ = 1024
RC_NORM = 256
TM_LAT = 512
TM_OUT = 512
TQ_DIFF = 256
TQ_MLA = 512
NH_DIFF = 1
NH_MLA = 1
TRIPS_DIFF = ((11, 12), (1, 4))
TRIPS_MLA = ((3, 12),)
Q_SLOTS = 8
QK_LEAD = 2
S_SLOTS = 4

DV_AUG = MLA_V_DIM + 16


def _rms_scale(xf):
    return lax.rsqrt(jnp.mean(xf * xf, axis=-1, keepdims=True) + EPS)


def _qkv_proj_kernel(x_ref, g_ref, w_ref, cs_ref, o_ref, h_ref):
    @pl.when(pl.program_id(1) == 0)
    def _():
        def body(c, carry):
            rows = pl.ds(pl.multiple_of(c * RC_NORM, RC_NORM), RC_NORM)
            xf = x_ref[rows, :]
            h_ref[rows, :] = (xf * _rms_scale(xf) * g_ref[...]).astype(BF16)
            return carry
        lax.fori_loop(0, TM_PROJ // RC_NORM, body, 0)

    acc = jnp.dot(h_ref[...], w_ref[...], preferred_element_type=F32)
    o_ref[...] = (acc * cs_ref[...]).astype(BF16)


def _qkv_proj(x2, g, w_qkv, colscale):
    m = x2.shape[0]
    n = w_qkv.shape[1]
    return pl.pallas_call(
        _qkv_proj_kernel,
        grid=(m // TM_PROJ, n // TN_PROJ),
        in_specs=[
            pl.BlockSpec((TM_PROJ, D_MODEL), lambda i, j: (i, 0)),
            pl.BlockSpec((1, D_MODEL), lambda i, j: (0, 0)),
            pl.BlockSpec((D_MODEL, TN_PROJ), lambda i, j: (0, j)),
            pl.BlockSpec((1, TN_PROJ), lambda i, j: (0, j)),
        ],
        out_specs=[pl.BlockSpec((TM_PROJ, TN_PROJ), lambda i, j: (i, j)),
                   pl.BlockSpec((TM_PROJ, D_MODEL), lambda i, j: (i, 0))],
        out_shape=[jax.ShapeDtypeStruct((m, n), BF16),
                   jax.ShapeDtypeStruct((m, D_MODEL), BF16)],
        compiler_params=pltpu.CompilerParams(
            dimension_semantics=("parallel", "arbitrary"), vmem_limit_bytes=VMEM_LIMIT),
        name="qkv_proj",
    )(x2, g, w_qkv, colscale)


def _gate_proj_kernel(h_ref, w_ref, o_ref):
    acc = jnp.dot(h_ref[...], w_ref[...], preferred_element_type=F32)
    o_ref[...] = (acc * jax.nn.sigmoid(acc)).astype(BF16)


def _gate_proj(h, w_gate):
    m = h.shape[0]
    n = w_gate.shape[1]
    return pl.pallas_call(
        _gate_proj_kernel,
        grid=(m // TM_PROJ, n // TN_PROJ),
        in_specs=[
            pl.BlockSpec((TM_PROJ, D_MODEL), lambda i, j: (i, 0)),
            pl.BlockSpec((D_MODEL, TN_PROJ), lambda i, j: (0, j)),
        ],
        out_specs=pl.BlockSpec((TM_PROJ, TN_PROJ), lambda i, j: (i, j)),
        out_shape=jax.ShapeDtypeStruct((m, n), BF16),
        compiler_params=pltpu.CompilerParams(
            dimension_semantics=("parallel", "arbitrary"), vmem_limit_bytes=VMEM_LIMIT),
        name="gate_proj",
    )(h, w_gate)


def _latent_kernel(h_ref, wl_ref, gq_ref, gkv_ref, wq_ref, wkv_ref, cos_ref, sin_ref,
                   mq_ref, mk_ref, qn_ref, qr_ref, kv_ref, kr_ref, *, q_scale):
    lat = jnp.dot(h_ref[...], wl_ref[...], preferred_element_type=F32)
    cq = lat[:, :MLA_Q_LORA]
    ckv = lat[:, MLA_Q_LORA:MLA_Q_LORA + MLA_KV_LORA]
    ka = lat[:, MLA_Q_LORA + MLA_KV_LORA:MLA_Q_LORA + MLA_KV_LORA + LANE]
    kb = lat[:, MLA_Q_LORA + MLA_KV_LORA + LANE:]
    cos = cos_ref[...]
    sin = sin_ref[...]
    mq = mq_ref[...]

    cqn = (cq * _rms_scale(cq) * gq_ref[...]).astype(BF16)
    qall = jnp.dot(cqn, wq_ref[...], preferred_element_type=F32) * q_scale
    nope_w = MLA_HEADS * MLA_NOPE
    qn_ref[...] = qall[:, :nope_w].astype(BF16)
    for hh in range(MLA_HEADS):
        a = qall[:, nope_w + hh * LANE:nope_w + (hh + 1) * LANE]
        b = qall[:, 2 * nope_w + hh * LANE:2 * nope_w + (hh + 1) * LANE]
        qr_ref[:, hh * LANE:(hh + 1) * LANE] = (a * cos + b * sin + mq).astype(BF16)

    ckvn = (ckv * _rms_scale(ckv) * gkv_ref[...]).astype(BF16)
    kv_ref[...] = jnp.dot(ckvn, wkv_ref[...], preferred_element_type=F32).astype(BF16)
    kr_ref[...] = (ka * cos + kb * sin + mk_ref[...]).astype(BF16)


def _latent(h, w_lat, gq, gkv, w_q2, w_kv2, pos_tables, seq):
    m = h.shape[0]
    seq_tiles = seq // TM_LAT
    const = lambda i: (0, 0)
    row = lambda i: (i, 0)
    pos = lambda i: (i % seq_tiles, 0)
    q_scale = (MLA_NOPE + MLA_ROPE) ** -0.5 * LOG2E
    return pl.pallas_call(
        functools.partial(_latent_kernel, q_scale=q_scale),
        grid=(m // TM_LAT,),
        in_specs=[
            pl.BlockSpec((TM_LAT, D_MODEL), row),
            pl.BlockSpec(w_lat.shape, const),
            pl.BlockSpec((1, MLA_Q_LORA), const),
            pl.BlockSpec((1, MLA_KV_LORA), const),
            pl.BlockSpec(w_q2.shape, const),
            pl.BlockSpec(w_kv2.shape, const),
        ] + [pl.BlockSpec((TM_LAT, LANE), pos)] * len(pos_tables),
        out_specs=[
            pl.BlockSpec((TM_LAT, MLA_HEADS * LANE), row),
            pl.BlockSpec((TM_LAT, MLA_HEADS * LANE), row),
            pl.BlockSpec((TM_LAT, w_kv2.shape[1]), row),
            pl.BlockSpec((TM_LAT, LANE), row),
        ],
        out_shape=[
            jax.ShapeDtypeStruct((m, MLA_HEADS * LANE), BF16),
            jax.ShapeDtypeStruct((m, MLA_HEADS * LANE), BF16),
            jax.ShapeDtypeStruct((m, w_kv2.shape[1]), BF16),
            jax.ShapeDtypeStruct((m, LANE), BF16),
        ],
        compiler_params=pltpu.CompilerParams(
            dimension_semantics=("parallel",), vmem_limit_bytes=VMEM_LIMIT),
        name="latent_proj",
    )(h, w_lat, gq, gkv, w_q2, w_kv2, *pos_tables)


def _attn_pipeline(n_q, trip_plan, heads, qk_operands, vt_tile, bias_tile, finalize,
                   s_buf, mt_buf, p_buf, a_buf, m_ref, acc_ref):
    last = n_q - 1
    total = n_q * (n_q + 1) // 2

    def advance(q, t):
        end = t >= q
        return (jnp.where(end, jnp.minimum(q + 1, last), q),
                jnp.where(end, jnp.where(q >= last, t, 0), t + 1))

    def qk_stage(qi, t, slot):
        for hh in heads:
            kt, qt = qk_operands(hh, qi, t)
            s = jnp.dot(kt, qt, preferred_element_type=F32)
            bias = bias_tile(hh, qi, t)
            if bias is not None:
                s = s + bias
            s_buf[hh, slot] = s
            mt_buf[hh, slot] = jnp.max(s, axis=0, keepdims=True)

    def sm_stage(qi, t, s_slot, p_slot):
        for hh in heads:
            s = s_buf[hh, s_slot]
            m_old = jnp.where(t == 0, NEG, m_ref[hh])
            m_new = jnp.maximum(m_old, mt_buf[hh, s_slot])
            m_ref[hh] = m_new
            p_buf[hh, p_slot] = jnp.exp2(s - m_new).astype(BF16)
            a_buf[hh, p_slot] = jnp.exp2(m_old - m_new)

    def pv_stage(qi, t, p_slot):
        for hh in heads:
            pv = jnp.dot(vt_tile(hh, t), p_buf[hh, p_slot], preferred_element_type=F32)
            qs = qi & (Q_SLOTS - 1)
            acc_ref[hh, qs] = a_buf[hh, p_slot] * acc_ref[hh, qs] + pv

    def finalize_all(qi):
        for hh in heads:
            finalize(hh, qi, acc_ref[hh, qi & (Q_SLOTS - 1)])

    m_ref[...] = jnp.full(m_ref.shape, NEG, F32)
    acc_ref[...] = jnp.zeros(acc_ref.shape, F32)
    p_buf[...] = jnp.zeros(p_buf.shape, BF16)
    a_buf[...] = jnp.zeros(a_buf.shape, F32)

    zero = jnp.int32(0)
    ahead = [(zero, zero)]
    for _ in range(QK_LEAD):
        ahead.append(advance(*ahead[-1]))
    for i in range(QK_LEAD):
        qk_stage(*ahead[i], i)

    def step(carry, j):
        lead, cur, prv = carry[:QK_LEAD], carry[QK_LEAD], carry[QK_LEAD + 1]
        pv_stage(*prv, (j + 1) % 2)
        qk_stage(*lead[0], (j + QK_LEAD) % S_SLOTS)
        sm_stage(*cur, j % S_SLOTS, j % 2)
        return (advance(*lead[0]),) + lead + (cur,), (prv[1] == prv[0], prv[0])

    reuse_distance = (Q_SLOTS - 1) * (Q_SLOTS + 2) // 2 + 1
    assert sum(n * steps for n, steps in trip_plan) == total
    assert all(steps % S_SLOTS == 0 and steps < reuse_distance for _, steps in trip_plan)

    def make_body(steps):
        def body(_, carry):
            done = []
            for j in range(steps):
                carry, d = step(carry, j)
                done.append(d)
            for ended, q in done:
                @pl.when(ended)
                def _():
                    finalize_all(q)
            return carry
        return body

    carry = tuple(reversed(ahead)) + ((zero, jnp.int32(1)),)
    for n_trips, steps in trip_plan:
        carry = lax.fori_loop(0, n_trips, make_body(steps), carry)
    pv_stage(*carry[-1], (total - 1) % 2)
    finalize_all(carry[-1][0])


def _attn_scratch(nh, n_kv, tk, n):
    return [pltpu.VMEM((nh, n_kv, DV_AUG, tk), BF16),
            pltpu.VMEM((nh, S_SLOTS, tk, n), F32),
            pltpu.VMEM((nh, S_SLOTS, 1, n), F32),
            pltpu.VMEM((nh, 2, tk, n), BF16),
            pltpu.VMEM((nh, 2, 1, n), F32),
            pltpu.VMEM((nh, 1, n), F32),
            pltpu.VMEM((nh, Q_SLOTS, DV_AUG, n), F32)]


def _fill_vt(vt_buf, hh, t, v):
    dv, tk = v.shape[1], v.shape[0]
    vt_buf[hh, t, :dv, :] = v.T
    row = lax.broadcasted_iota(jnp.int32, (DV_AUG - dv, tk), 0)
    vt_buf[hh, t, dv:, :] = jnp.where(row == 0, 1.0, 0.0).astype(BF16)


def _diff_attn_kernel(lam_ref, q_ref, k_ref, v_ref, gate_ref, tab_ref, g_ref, o_ref,
                      qz_buf, vt_buf, *scratch, out_scale):
    tq = TQ_DIFF
    hd = 2 * DIFF_HEAD_DIM
    assert hd == MLA_V_DIM
    n_q = q_ref.shape[1] // tq
    heads = range(NH_DIFF)
    lam = lam_ref[0, 0]

    def rows_of(i):
        return pl.ds(pl.multiple_of(i * tq, tq), tq)

    def cols_of(hh):
        return slice(hh * hd, (hh + 1) * hd)

    feat = lax.broadcasted_iota(jnp.int32, (hd, tq), 0)

    def prep(qi, carry):
        for hh in heads:
            qb = q_ref[0, rows_of(qi), cols_of(hh)].T
            zero = jnp.zeros_like(qb)
            qz_buf[hh, qi, :, :tq] = jnp.where(feat < DIFF_HEAD_DIM, qb, zero)
            qz_buf[hh, qi, :, tq:] = jnp.where(feat >= DIFF_HEAD_DIM, qb, zero)
            _fill_vt(vt_buf, hh, qi, v_ref[0, rows_of(qi), cols_of(hh)])
        return carry
    lax.fori_loop(0, n_q, prep, 0)

    def qk_operands(hh, qi, t):
        return k_ref[0, rows_of(t), cols_of(hh)], qz_buf[hh, qi]

    def vt_tile(hh, t):
        return vt_buf[hh, t]

    def bias_tile(hh, qi, t):
        return tab_ref[hh, jnp.maximum(t - qi + 2, 0)]

    def finalize(hh, qi, acc):
        inv = 1.0 / acc[hd:hd + 1]
        o_t = acc[:hd, :tq] * inv[:, :tq] - lam * (acc[:hd, tq:] * inv[:, tq:])
        o_t = o_t * lax.rsqrt(jnp.mean(o_t * o_t, axis=0, keepdims=True) + EPS)
        o = o_t.T * g_ref[...] * out_scale
        gate = gate_ref[0, rows_of(qi), cols_of(hh)].astype(F32)
        o_ref[0, rows_of(qi), cols_of(hh)] = (o * gate).astype(BF16)

    _attn_pipeline(n_q, TRIPS_DIFF, heads, qk_operands, vt_tile, bias_tile, finalize, *scratch)


def _diff_attn(qkv3, gates3, lam, tab, g, lam_init):
    b, s, _ = qkv3.shape
    tq = TQ_DIFF
    hd = 2 * DIFF_HEAD_DIM
    nh = NH_DIFF
    blk = nh * hd
    kcol, vcol = DIFF_WIDTH // blk, 2 * DIFF_WIDTH // blk
    seq_blk = lambda off: pl.BlockSpec((1, s, blk), lambda bi, h: (bi, 0, off + h))
    return pl.pallas_call(
        functools.partial(_diff_attn_kernel, out_scale=1.0 - lam_init),
        grid=(b, DIFF_HEADS // nh),
        in_specs=[
            pl.BlockSpec(memory_space=pltpu.SMEM),
            seq_blk(0), seq_blk(kcol), seq_blk(vcol), seq_blk(0),
            pl.BlockSpec((nh, 3, tq, 2 * tq), lambda bi, h: (h, 0, 0, 0)),
            pl.BlockSpec((1, hd), lambda bi, h: (0, 0)),
        ],
        out_specs=seq_blk(0),
        out_shape=jax.ShapeDtypeStruct((b, s, DIFF_WIDTH), BF16),
        scratch_shapes=[pltpu.VMEM((nh, s // tq, hd, 2 * tq), BF16)]
        + _attn_scratch(nh, s // tq, tq, 2 * tq),
        compiler_params=pltpu.CompilerParams(
            dimension_semantics=("parallel", "arbitrary"), vmem_limit_bytes=VMEM_LIMIT),
        name="diff_attn",
    )(lam, qkv3, qkv3, qkv3, gates3, tab, g)


def _mla_attn_kernel(qn_ref, qr_ref, kv_ref, kr_ref, gate_ref, o_ref, qt_buf, vt_buf, *scratch):
    tq = TQ_MLA
    n_q = qn_ref.shape[1] // tq
    heads = range(NH_MLA)

    def rows_of(i):
        return pl.ds(pl.multiple_of(i * tq, tq), tq)

    def cols_of(hh):
        return slice(hh * LANE, (hh + 1) * LANE)

    def prep(qi, carry):
        for hh in heads:
            qt_buf[hh, qi, :LANE, :] = qn_ref[0, rows_of(qi), cols_of(hh)].T
            qt_buf[hh, qi, LANE:, :] = qr_ref[0, rows_of(qi), cols_of(hh)].T
            _fill_vt(vt_buf, hh, qi, kv_ref[0, rows_of(qi), cols_of(2 * hh + 1)])
        return carry
    lax.fori_loop(0, n_q, prep, 0)

    def qk_operands(hh, qi, t):
        kt = jnp.concatenate([kv_ref[0, rows_of(t), cols_of(2 * hh)], kr_ref[0, rows_of(t), :]],
                             axis=1)
        return kt, qt_buf[hh, qi]

    def vt_tile(hh, t):
        return vt_buf[hh, t]

    def finalize(hh, qi, acc):
        dv = MLA_V_DIM
        o = (acc[:dv] * (1.0 / acc[dv:dv + 1])).T
        gate = gate_ref[0, rows_of(qi), cols_of(hh)].astype(F32)
        o_ref[0, rows_of(qi), cols_of(hh)] = (o * gate).astype(BF16)

    _attn_pipeline(n_q, TRIPS_MLA, heads, qk_operands, vt_tile, lambda hh, qi, t: None, finalize,
                   *scratch)


def _mla_attn(qn3, qr3, kv3, kr3, gates3):
    b, s, _ = qn3.shape
    tq = TQ_MLA
    nh = NH_MLA
    blk = nh * LANE
    gcol = DIFF_WIDTH // blk
    seq_blk = lambda w, off: pl.BlockSpec((1, s, w), lambda bi, h: (bi, 0, off + h))
    return pl.pallas_call(
        _mla_attn_kernel,
        grid=(b, MLA_HEADS // nh),
        in_specs=[
            seq_blk(blk, 0), seq_blk(blk, 0), seq_blk(2 * blk, 0),
            pl.BlockSpec((1, s, LANE), lambda bi, h: (bi, 0, 0)),
            seq_blk(blk, gcol),
        ],
        out_specs=seq_blk(blk, 0),
        out_shape=jax.ShapeDtypeStruct((b, s, MLA_WIDTH), BF16),
        scratch_shapes=[pltpu.VMEM((nh, s // tq, 2 * LANE, tq), BF16)]
        + _attn_scratch(nh, s // tq, tq, tq),
        compiler_params=pltpu.CompilerParams(
            dimension_semantics=("parallel", "arbitrary"), vmem_limit_bytes=VMEM_LIMIT),
        name="mla_attn",
    )(qn3, qr3, kv3, kr3, gates3)


def _out_proj_kernel(x_ref, ya_ref, yb_ref, w_ref, gf_ref, o_ref, *, final):
    acc = jnp.dot(ya_ref[...], w_ref[:DIFF_WIDTH, :], preferred_element_type=F32)
    acc = acc + jnp.dot(yb_ref[...], w_ref[DIFF_WIDTH:, :], preferred_element_type=F32)
    xn = x_ref[...] + acc
    if final:
        xn = xn * _rms_scale(xn) * gf_ref[...]
    o_ref[...] = xn


def _out_proj(x2, ya, yb, w_out, gf, final):
    m = x2.shape[0]
    row = lambda i: (i, 0)
    const = lambda i: (0, 0)
    return pl.pallas_call(
        functools.partial(_out_proj_kernel, final=final),
        grid=(m // TM_OUT,),
        in_specs=[
            pl.BlockSpec((TM_OUT, D_MODEL), row),
            pl.BlockSpec((TM_OUT, DIFF_WIDTH), row),
            pl.BlockSpec((TM_OUT, MLA_WIDTH), row),
            pl.BlockSpec((MIX_WIDTH, D_MODEL), const),
            pl.BlockSpec((1, D_MODEL), const),
        ],
        out_specs=pl.BlockSpec((TM_OUT, D_MODEL), row),
        out_shape=jax.ShapeDtypeStruct((m, D_MODEL), F32),
        compiler_params=pltpu.CompilerParams(
            dimension_semantics=("parallel",), vmem_limit_bytes=VMEM_LIMIT),
        name="out_proj",
    )(x2, ya, yb, w_out, gf)


def _rel_bucket(rel):
    nb = REL_BUCKETS // 2
    max_exact = nb // 2
    ret = (rel > 0).astype(jnp.int32) * nb
    n = jnp.abs(rel)
    nf = jnp.maximum(n, 1).astype(F32)
    large = max_exact + (jnp.log(nf / max_exact) / math.log(REL_MAX_DIST / max_exact)
                         * (nb - max_exact)).astype(jnp.int32)
    large = jnp.minimum(large, nb - 1)
    return ret + jnp.where(n < max_exact, n, large)


def _bias_tables(rel_bias):
    tq = TQ_DIFF
    kk = jnp.arange(2 * tq, dtype=jnp.int32)[:, None]
    qq = jnp.arange(tq, dtype=jnp.int32)[None, :]
    onehot = (_rel_bucket(kk - tq - qq)[..., None] == jnp.arange(REL_BUCKETS)).astype(F32)
    bias = jnp.einsum('kqb,bc->kqc', onehot, rel_bias.astype(F32),
                      precision=lax.Precision.HIGHEST) * LOG2E
    allowed = (kk < tq) | (((kk - tq) // CHUNK) <= (qq // CHUNK))
    bias = jnp.where(allowed[..., None], bias, NEG)
    near = bias.reshape(2, tq, tq, DIFF_HEADS, 2).transpose(3, 0, 1, 4, 2)
    near = near.reshape(DIFF_HEADS, 2, tq, 2 * tq)
    far = rel_bias[_rel_bucket(jnp.asarray(-2 * REL_MAX_DIST, jnp.int32))].astype(F32) * LOG2E
    far = jnp.repeat(far.reshape(DIFF_HEADS, 2), tq, axis=1)
    far = jnp.broadcast_to(far[:, None, None, :], (DIFF_HEADS, 1, tq, 2 * tq))
    return jnp.concatenate([far, near], axis=1)


def _pos_tables(seq):
    pos = jnp.arange(seq, dtype=F32)
    inv_freq = ROPE_BASE ** (-jnp.arange(0, MLA_ROPE, 2, dtype=F32) / MLA_ROPE)
    ang = pos[:, None] * inv_freq[None, :]
    cos, sin = jnp.cos(ang), jnp.sin(ang)
    pad = jnp.zeros((seq, LANE - MLA_ROPE), F32)
    n_chunks = seq // CHUNK
    assert n_chunks <= LANE - MLA_ROPE
    chunk = (jnp.arange(seq, dtype=jnp.int32) // CHUNK)[:, None]
    cid = jnp.arange(LANE, dtype=jnp.int32)[None, :] - MLA_ROPE
    valid = (cid >= 0) & (cid < n_chunks)
    mq = jnp.where(valid & (cid > chunk), NEG, 0.0).astype(F32)
    mk = jnp.where(valid & (cid == chunk), 1.0, 0.0).astype(F32)
    return (jnp.concatenate([cos, cos, pad], axis=1),
            jnp.concatenate([-sin, sin, pad], axis=1), mq, mk)


def _swap_halves(w):
    half = w.shape[-1] // 2
    return jnp.concatenate([w[..., half:], w[..., :half]], axis=-1)


def _pad_lanes(w):
    return jnp.pad(w, [(0, 0)] * (w.ndim - 1) + [(0, LANE - w.shape[-1])])


def kernel(x, norm_g, w_in, diff_lambda, diff_subln_g, mla_q_norm_g, mla_w_q_b,
           mla_kv_norm_g, mla_w_kv_b, w_out, rel_bias, final_norm_g):
    b, s, d = x.shape
    assert d == D_MODEL and s % TQ_MLA == 0 and s >= 2 * TQ_MLA and (b * s) % TM_PROJ == 0
    m = b * s
    pos_tables = _pos_tables(s)
    tab = _bias_tables(rel_bias)
    colscale = jnp.concatenate([
        jnp.full((1, DIFF_WIDTH), DIFF_HEAD_DIM ** -0.5 * LOG2E, F32),
        jnp.ones((1, QKV_WIDTH - DIFF_WIDTH), F32)], axis=1)
    gf = final_norm_g.reshape(1, D_MODEL)

    x2 = x.reshape(m, d)
    for l in range(DEPTH):
        wl = w_in[l]
        w_qkv = wl[:, :QKV_WIDTH].astype(BF16)
        w_gate = wl[:, GATE_OFF:].astype(BF16)
        w_kr = wl[:, GATE_OFF - MLA_ROPE:GATE_OFF]
        w_lat = jnp.concatenate([wl[:, LAT_OFF:GATE_OFF - MLA_ROPE], _pad_lanes(w_kr),
                                 _pad_lanes(_swap_halves(w_kr))], axis=1).astype(BF16)
        wq = mla_w_q_b[l].reshape(MLA_Q_LORA, MLA_HEADS, MLA_NOPE + MLA_ROPE)
        wq_rope = wq[:, :, MLA_NOPE:]
        w_q2 = jnp.concatenate([
            wq[:, :, :MLA_NOPE].reshape(MLA_Q_LORA, -1),
            _pad_lanes(wq_rope).reshape(MLA_Q_LORA, -1),
            _pad_lanes(_swap_halves(wq_rope)).reshape(MLA_Q_LORA, -1)], axis=1).astype(BF16)
        w_kv2 = mla_w_kv_b[l].astype(BF16)
        g = norm_g[l].reshape(1, D_MODEL)

        lam_init = 0.8 - 0.6 * math.exp(-0.3 * l)
        lp = diff_lambda[l].astype(F32)
        lam = jnp.exp(jnp.sum(lp[0] * lp[1])) - jnp.exp(jnp.sum(lp[2] * lp[3])) + lam_init

        qkv, h = _qkv_proj(x2, g, w_qkv, colscale)
        gates3 = _gate_proj(h, w_gate).reshape(b, s, MIX_WIDTH)
        qn, qr, kv, kr = _latent(h, w_lat, mla_q_norm_g[l].reshape(1, -1),
                                 mla_kv_norm_g[l].reshape(1, -1), w_q2, w_kv2, pos_tables, s)
        ya = _diff_attn(qkv.reshape(b, s, QKV_WIDTH), gates3, lam.reshape(1, 1), tab,
                        diff_subln_g[l].reshape(1, -1), lam_init)
        yb = _mla_attn(qn.reshape(b, s, -1), qr.reshape(b, s, -1), kv.reshape(b, s, -1),
                       kr.reshape(b, s, -1), gates3)
        x2 = _out_proj(x2, ya.reshape(m, -1), yb.reshape(m, -1), w_out[l].astype(BF16), gf,
                       final=(l == DEPTH - 1))
    return x2.reshape(b, s, d)
```

```python
import functools
import math

import jax
import jax.numpy as jnp
from jax import lax
from jax.experimental import pallas as pl
from jax.experimental.pallas import tpu as pltpu

D_MODEL = 2048
DEPTH = 2
CHUNK = 64
MIX_WIDTH = D_MODEL
DIFF_WIDTH = MIX_WIDTH // 2
MLA_WIDTH = MIX_WIDTH - DIFF_WIDTH
DIFF_HEAD_DIM = 64
DIFF_HEADS = DIFF_WIDTH // (2 * DIFF_HEAD_DIM)
MLA_V_DIM = 128
MLA_HEADS = MLA_WIDTH // MLA_V_DIM
MLA_NOPE = 128
MLA_ROPE = 64
MLA_Q_LORA = 512
MLA_KV_LORA = 256
ROPE_BASE = 10000.0
REL_BUCKETS = 32
REL_MAX_DIST = 128
EPS = 1e-6
NEG = -1e30
QKV_WIDTH = 3 * DIFF_WIDTH
LAT_OFF = QKV_WIDTH
GATE_OFF = LAT_OFF + MLA_Q_LORA + MLA_KV_LORA + MLA_ROPE

LOG2E = math.log2(math.e)
LANE = 128
VMEM_LIMIT = 56 * 1024 * 1024

BF16 = jnp.bfloat16
F32 = jnp.float32

TM_PROJ = 1024
TN_PROJ = 1024
RC_NORM = 256
TM_LAT = 512
TM_OUT = 512
TQ_DIFF = 256
TQ_MLA = 512
NH_DIFF = 1
NH_MLA = 1
TRIPS_DIFF = ((11, 12), (1, 4))
TRIPS_MLA = ((3, 12),)
Q_SLOTS = 8
QK_LEAD = 2
S_SLOTS = 4

DV_AUG = MLA_V_DIM + 16


def _rms_scale(xf):
    return lax.rsqrt(jnp.mean(xf * xf, axis=-1, keepdims=True) + EPS)


def _qkv_proj_kernel(x_ref, g_ref, w_ref, cs_ref, o_ref, h_ref):
    @pl.when(pl.program_id(1) == 0)
    def _():
        def body(c, carry):
            rows = pl.ds(pl.multiple_of(c * RC_NORM, RC_NORM), RC_NORM)
            xf = x_ref[rows, :]
            h_ref[rows, :] = (xf * _rms_scale(xf) * g_ref[...]).astype(BF16)
            return carry
        lax.fori_loop(0, TM_PROJ // RC_NORM, body, 0)

    acc = jnp.dot(h_ref[...], w_ref[...], preferred_element_type=F32)
    o_ref[...] = (acc * cs_ref[...]).astype(BF16)


def _qkv_proj(x2, g, w_qkv, colscale):
    m = x2.shape[0]
    n = w_qkv.shape[1]
    return pl.pallas_call(
        _qkv_proj_kernel,
        grid=(m // TM_PROJ, n // TN_PROJ),
        in_specs=[
            pl.BlockSpec((TM_PROJ, D_MODEL), lambda i, j: (i, 0)),
            pl.BlockSpec((1, D_MODEL), lambda i, j: (0, 0)),
            pl.BlockSpec((D_MODEL, TN_PROJ), lambda i, j: (0, j)),
            pl.BlockSpec((1, TN_PROJ), lambda i, j: (0, j)),
        ],
        out_specs=[pl.BlockSpec((TM_PROJ, TN_PROJ), lambda i, j: (i, j)),
                   pl.BlockSpec((TM_PROJ, D_MODEL), lambda i, j: (i, 0))],
        out_shape=[jax.ShapeDtypeStruct((m, n), BF16),
                   jax.ShapeDtypeStruct((m, D_MODEL), BF16)],
        compiler_params=pltpu.CompilerParams(
            dimension_semantics=("parallel", "arbitrary"), vmem_limit_bytes=VMEM_LIMIT),
        name="qkv_proj",
    )(x2, g, w_qkv, colscale)


def _gate_proj_kernel(h_ref, w_ref, o_ref):
    acc = jnp.dot(h_ref[...], w_ref[...], preferred_element_type=F32)
    o_ref[...] = (acc * jax.nn.sigmoid(acc)).astype(BF16)


def _gate_proj(h, w_gate):
    m = h.shape[0]
    n = w_gate.shape[1]
    return pl.pallas_call(
        _gate_proj_kernel,
        grid=(m // TM_PROJ, n // TN_PROJ),
        in_specs=[
            pl.BlockSpec((TM_PROJ, D_MODEL), lambda i, j: (i, 0)),
            pl.BlockSpec((D_MODEL, TN_PROJ), lambda i, j: (0, j)),
        ],
        out_specs=pl.BlockSpec((TM_PROJ, TN_PROJ), lambda i, j: (i, j)),
        out_shape=jax.ShapeDtypeStruct((m, n), BF16),
        compiler_params=pltpu.CompilerParams(
            dimension_semantics=("parallel", "arbitrary"), vmem_limit_bytes=VMEM_LIMIT),
        name="gate_proj",
    )(h, w_gate)


def _latent_kernel(h_ref, wl_ref, gq_ref, gkv_ref, wq_ref, wkv_ref, cos_ref, sin_ref,
                   mq_ref, mk_ref, qn_ref, qr_ref, kv_ref, kr_ref, *, q_scale):
    lat = jnp.dot(h_ref[...], wl_ref[...], preferred_element_type=F32)
    cq = lat[:, :MLA_Q_LORA]
    ckv = lat[:, MLA_Q_LORA:MLA_Q_LORA + MLA_KV_LORA]
    ka = lat[:, MLA_Q_LORA + MLA_KV_LORA:MLA_Q_LORA + MLA_KV_LORA + LANE]
    kb = lat[:, MLA_Q_LORA + MLA_KV_LORA + LANE:]
    cos = cos_ref[...]
    sin = sin_ref[...]
    mq = mq_ref[...]

    cqn = (cq * _rms_scale(cq) * gq_ref[...]).astype(BF16)
    qall = jnp.dot(cqn, wq_ref[...], preferred_element_type=F32) * q_scale
    nope_w = MLA_HEADS * MLA_NOPE
    qn_ref[...] = qall[:, :nope_w].astype(BF16)
    for hh in range(MLA_HEADS):
        a = qall[:, nope_w + hh * LANE:nope_w + (hh + 1) * LANE]
        b = qall[:, 2 * nope_w + hh * LANE:2 * nope_w + (hh + 1) * LANE]
        qr_ref[:, hh * LANE:(hh + 1) * LANE] = (a * cos + b * sin + mq).astype(BF16)

    ckvn = (ckv * _rms_scale(ckv) * gkv_ref[...]).astype(BF16)
    kv_ref[...] = jnp.dot(ckvn, wkv_ref[...], preferred_element_type=F32).astype(BF16)
    kr_ref[...] = (ka * cos + kb * sin + mk_ref[...]).astype(BF16)


def _latent(h, w_lat, gq, gkv, w_q2, w_kv2, pos_tables, seq):
    m = h.shape[0]
    seq_tiles = seq // TM_LAT
    const = lambda i: (0, 0)
    row = lambda i: (i, 0)
    pos = lambda i: (i % seq_tiles, 0)
    q_scale = (MLA_NOPE + MLA_ROPE) ** -0.5 * LOG2E
    return pl.pallas_call(
        functools.partial(_latent_kernel, q_scale=q_scale),
        grid=(m // TM_LAT,),
        in_specs=[
            pl.BlockSpec((TM_LAT, D_MODEL), row),
            pl.BlockSpec(w_lat.shape, const),
            pl.BlockSpec((1, MLA_Q_LORA), const),
            pl.BlockSpec((1, MLA_KV_LORA), const),
            pl.BlockSpec(w_q2.shape, const),
            pl.BlockSpec(w_kv2.shape, const),
        ] + [pl.BlockSpec((TM_LAT, LANE), pos)] * len(pos_tables),
        out_specs=[
            pl.BlockSpec((TM_LAT, MLA_HEADS * LANE), row),
            pl.BlockSpec((TM_LAT, MLA_HEADS * LANE), row),
            pl.BlockSpec((TM_LAT, w_kv2.shape[1]), row),
            pl.BlockSpec((TM_LAT, LANE), row),
        ],
        out_shape=[
            jax.ShapeDtypeStruct((m, MLA_HEADS * LANE), BF16),
            jax.ShapeDtypeStruct((m, MLA_HEADS * LANE), BF16),
            jax.ShapeDtypeStruct((m, w_kv2.shape[1]), BF16),
            jax.ShapeDtypeStruct((m, LANE), BF16),
        ],
        compiler_params=pltpu.CompilerParams(
            dimension_semantics=("parallel",), vmem_limit_bytes=VMEM_LIMIT),
        name="latent_proj",
    )(h, w_lat, gq, gkv, w_q2, w_kv2, *pos_tables)


def _attn_pipeline(n_q, trip_plan, heads, qk_operands, vt_tile, bias_tile, finalize,
                   s_buf, mt_buf, m_ref, acc_ref):
    last = n_q - 1
    total = n_q * (n_q + 1) // 2

    def advance(q, t):
        end = t >= q
        return (jnp.where(end, jnp.minimum(q + 1, last), q),
                jnp.where(end, jnp.where(q >= last, t, 0), t + 1))

    def qk_stage(qi, t, slot):
        for hh in heads:
            kt, qt = qk_operands(hh, qi, t)
            s = jnp.dot(kt, qt, preferred_element_type=F32)
            bias = bias_tile(hh, qi, t)
            if bias is not None:
                s = s + bias
            s_buf[hh, slot] = s
            mt_buf[hh, slot] = jnp.max(s, axis=0, keepdims=True)

    def sm_pv_stage(qi, t, s_slot):
        for hh in heads:
            s = s_buf[hh, s_slot]
            m_old = jnp.where(t == 0, NEG, m_ref[hh])
            m_new = jnp.maximum(m_old, mt_buf[hh, s_slot])
            m_ref[hh] = m_new
            p = jnp.exp2(s - m_new).astype(BF16)
            pv = jnp.dot(vt_tile(hh, t), p, preferred_element_type=F32)
            qs = qi & (Q_SLOTS - 1)
            acc_ref[hh, qs] = jnp.exp2(m_old - m_new) * acc_ref[hh, qs] + pv

    def finalize_all(qi):
        for hh in heads:
            finalize(hh, qi, acc_ref[hh, qi & (Q_SLOTS - 1)])

    m_ref[...] = jnp.full(m_ref.shape, NEG, F32)
    acc_ref[...] = jnp.zeros(acc_ref.shape, F32)

    zero = jnp.int32(0)
    ahead = [(zero, zero)]
    for _ in range(QK_LEAD):
        ahead.append(advance(*ahead[-1]))
    for i in range(QK_LEAD):
        qk_stage(*ahead[i], i)

    def step(carry, j):
        lead, cur = carry[:QK_LEAD], carry[QK_LEAD]
        qk_stage(*lead[0], (j + QK_LEAD) % S_SLOTS)
        sm_pv_stage(*cur, j % S_SLOTS)
        return (advance(*lead[0]),) + lead, (cur[1] == cur[0], cur[0])

    reuse_distance = (Q_SLOTS - 1) * (Q_SLOTS + 2) // 2 + 1
    assert sum(n * steps for n, steps in trip_plan) == total
    assert all(steps % S_SLOTS == 0 and steps < reuse_distance for _, steps in trip_plan)

    def make_body(steps):
        def body(_, carry):
            done = []
            for j in range(steps):
                carry, d = step(carry, j)
                done.append(d)
            for ended, q in done:
                @pl.when(ended)
                def _():
                    finalize_all(q)
            return carry
        return body

    carry = tuple(reversed(ahead))
    for n_trips, steps in trip_plan:
        carry = lax.fori_loop(0, n_trips, make_body(steps), carry)


def _attn_scratch(nh, n_kv, tk, n):
    return [pltpu.VMEM((nh, n_kv, DV_AUG, tk), BF16),
            pltpu.VMEM((nh, S_SLOTS, tk, n), F32),
            pltpu.VMEM((nh, S_SLOTS, 1, n), F32),
            pltpu.VMEM((nh, 1, n), F32),
            pltpu.VMEM((nh, Q_SLOTS, DV_AUG, n), F32)]


def _fill_vt(vt_buf, hh, t, v):
    dv, tk = v.shape[1], v.shape[0]
    vt_buf[hh, t, :dv, :] = v.T
    row = lax.broadcasted_iota(jnp.int32, (DV_AUG - dv, tk), 0)
    vt_buf[hh, t, dv:, :] = jnp.where(row == 0, 1.0, 0.0).astype(BF16)


def _diff_attn_kernel(lam_ref, q_ref, k_ref, v_ref, gate_ref, tab_ref, g_ref, o_ref,
                      qz_buf, vt_buf, *scratch, out_scale):
    tq = TQ_DIFF
    hd = 2 * DIFF_HEAD_DIM
    assert hd == MLA_V_DIM
    n_q = q_ref.shape[1] // tq
    heads = range(NH_DIFF)
    lam = lam_ref[0, 0]

    def rows_of(i):
        return pl.ds(pl.multiple_of(i * tq, tq), tq)

    def cols_of(hh):
        return slice(hh * hd, (hh + 1) * hd)

    feat = lax.broadcasted_iota(jnp.int32, (hd, tq), 0)

    def prep(qi, carry):
        for hh in heads:
            qb = q_ref[0, rows_of(qi), cols_of(hh)].T
            zero = jnp.zeros_like(qb)
            qz_buf[hh, qi, :, :tq] = jnp.where(feat < DIFF_HEAD_DIM, qb, zero)
            qz_buf[hh, qi, :, tq:] = jnp.where(feat >= DIFF_HEAD_DIM, qb, zero)
            _fill_vt(vt_buf, hh, qi, v_ref[0, rows_of(qi), cols_of(hh)])
        return carry
    lax.fori_loop(0, n_q, prep, 0)

    def qk_operands(hh, qi, t):
        return k_ref[0, rows_of(t), cols_of(hh)], qz_buf[hh, qi]

    def vt_tile(hh, t):
        return vt_buf[hh, t]

    def bias_tile(hh, qi, t):
        return tab_ref[hh, jnp.maximum(t - qi + 2, 0)]

    def finalize(hh, qi, acc):
        inv = 1.0 / acc[hd:hd + 1]
        o_t = acc[:hd, :tq] * inv[:, :tq] - lam * (acc[:hd, tq:] * inv[:, tq:])
        o_t = o_t * lax.rsqrt(jnp.mean(o_t * o_t, axis=0, keepdims=True) + EPS)
        o = o_t.T * g_ref[...] * out_scale
        gate = gate_ref[0, rows_of(qi), cols_of(hh)].astype(F32)
        o_ref[0, rows_of(qi), cols_of(hh)] = (o * gate).astype(BF16)

    _attn_pipeline(n_q, TRIPS_DIFF, heads, qk_operands, vt_tile, bias_tile, finalize, *scratch)


def _diff_attn(qkv3, gates3, lam, tab, g, lam_init):
    b, s, _ = qkv3.shape
    tq = TQ_DIFF
    hd = 2 * DIFF_HEAD_DIM
    nh = NH_DIFF
    blk = nh * hd
    kcol, vcol = DIFF_WIDTH // blk, 2 * DIFF_WIDTH // blk
    seq_blk = lambda off: pl.BlockSpec((1, s, blk), lambda bi, h: (bi, 0, off + h))
    return pl.pallas_call(
        functools.partial(_diff_attn_kernel, out_scale=1.0 - lam_init),
        grid=(b, DIFF_HEADS // nh),
        in_specs=[
            pl.BlockSpec(memory_space=pltpu.SMEM),
            seq_blk(0), seq_blk(kcol), seq_blk(vcol), seq_blk(0),
            pl.BlockSpec((nh, 3, tq, 2 * tq), lambda bi, h: (h, 0, 0, 0)),
            pl.BlockSpec((1, hd), lambda bi, h: (0, 0)),
        ],
        out_specs=seq_blk(0),
        out_shape=jax.ShapeDtypeStruct((b, s, DIFF_WIDTH), BF16),
        scratch_shapes=[pltpu.VMEM((nh, s // tq, hd, 2 * tq), BF16)]
        + _attn_scratch(nh, s // tq, tq, 2 * tq),
        compiler_params=pltpu.CompilerParams(
            dimension_semantics=("parallel", "arbitrary"), vmem_limit_bytes=VMEM_LIMIT),
        name="diff_attn",
    )(lam, qkv3, qkv3, qkv3, gates3, tab, g)


def _mla_attn_kernel(qn_ref, qr_ref, kv_ref, kr_ref, gate_ref, o_ref, qt_buf, vt_buf, *scratch):
    tq = TQ_MLA
    n_q = qn_ref.shape[1] // tq
    heads = range(NH_MLA)

    def rows_of(i):
        return pl.ds(pl.multiple_of(i * tq, tq), tq)

    def cols_of(hh):
        return slice(hh * LANE, (hh + 1) * LANE)

    def prep(qi, carry):
        for hh in heads:
            qt_buf[hh, qi, :LANE, :] = qn_ref[0, rows_of(qi), cols_of(hh)].T
            qt_buf[hh, qi, LANE:, :] = qr_ref[0, rows_of(qi), cols_of(hh)].T
            _fill_vt(vt_buf, hh, qi, kv_ref[0, rows_of(qi), cols_of(2 * hh + 1)])
        return carry
    lax.fori_loop(0, n_q, prep, 0)

    def qk_operands(hh, qi, t):
        kt = jnp.concatenate([kv_ref[0, rows_of(t), cols_of(2 * hh)], kr_ref[0, rows_of(t), :]],
                             axis=1)
        return kt, qt_buf[hh, qi]

    def vt_tile(hh, t):
        return vt_buf[hh, t]

    def finalize(hh, qi, acc):
        dv = MLA_V_DIM
        o = (acc[:dv] * (1.0 / acc[dv:dv + 1])).T
        gate = gate_ref[0, rows_of(qi), cols_of(hh)].astype(F32)
        o_ref[0, rows_of(qi), cols_of(hh)] = (o * gate).astype(BF16)

    _attn_pipeline(n_q, TRIPS_MLA, heads, qk_operands, vt_tile, lambda hh, qi, t: None, finalize,
                   *scratch)


def _mla_attn(qn3, qr3, kv3, kr3, gates3):
    b, s, _ = qn3.shape
    tq = TQ_MLA
    nh = NH_MLA
    blk = nh * LANE
    gcol = DIFF_WIDTH // blk
    seq_blk = lambda w, off: pl.BlockSpec((1, s, w), lambda bi, h: (bi, 0, off + h))
    return pl.pallas_call(
        _mla_attn_kernel,
        grid=(b, MLA_HEADS // nh),
        in_specs=[
            seq_blk(blk, 0), seq_blk(blk, 0), seq_blk(2 * blk, 0),
            pl.BlockSpec((1, s, LANE), lambda bi, h: (bi, 0, 0)),
            seq_blk(blk, gcol),
        ],
        out_specs=seq_blk(blk, 0),
        out_shape=jax.ShapeDtypeStruct((b, s, MLA_WIDTH), BF16),
        scratch_shapes=[pltpu.VMEM((nh, s // tq, 2 * LANE, tq), BF16)]
        + _attn_scratch(nh, s // tq, tq, tq),
        compiler_params=pltpu.CompilerParams(
            dimension_semantics=("parallel", "arbitrary"), vmem_limit_bytes=VMEM_LIMIT),
        name="mla_attn",
    )(qn3, qr3, kv3, kr3, gates3)


def _out_proj_kernel(x_ref, ya_ref, yb_ref, w_ref, gf_ref, o_ref, *, final):
    acc = jnp.dot(ya_ref[...], w_ref[:DIFF_WIDTH, :], preferred_element_type=F32)
    acc = acc + jnp.dot(yb_ref[...], w_ref[DIFF_WIDTH:, :], preferred_element_type=F32)
    xn = x_ref[...] + acc
    if final:
        xn = xn * _rms_scale(xn) * gf_ref[...]
    o_ref[...] = xn


def _out_proj(x2, ya, yb, w_out, gf, final):
    m = x2.shape[0]
    row = lambda i: (i, 0)
    const = lambda i: (0, 0)
    return pl.pallas_call(
        functools.partial(_out_proj_kernel, final=final),
        grid=(m // TM_OUT,),
        in_specs=[
            pl.BlockSpec((TM_OUT, D_MODEL), row),
            pl.BlockSpec((TM_OUT, DIFF_WIDTH), row),
            pl.BlockSpec((TM_OUT, MLA_WIDTH), row),
            pl.BlockSpec((MIX_WIDTH, D_MODEL), const),
            pl.BlockSpec((1, D_MODEL), const),
        ],
        out_specs=pl.BlockSpec((TM_OUT, D_MODEL), row),
        out_shape=jax.ShapeDtypeStruct((m, D_MODEL), F32),
        compiler_params=pltpu.CompilerParams(
            dimension_semantics=("parallel",), vmem_limit_bytes=VMEM_LIMIT),
        name="out_proj",
    )(x2, ya, yb, w_out, gf)


def _rel_bucket(rel):
    nb = REL_BUCKETS // 2
    max_exact = nb // 2
    ret = (rel > 0).astype(jnp.int32) * nb
    n = jnp.abs(rel)
    nf = jnp.maximum(n, 1).astype(F32)
    large = max_exact + (jnp.log(nf / max_exact) / math.log(REL_MAX_DIST / max_exact)
                         * (nb - max_exact)).astype(jnp.int32)
    large = jnp.minimum(large, nb - 1)
    return ret + jnp.where(n < max_exact, n, large)


def _bias_tables(rel_bias):
    tq = TQ_DIFF
    kk = jnp.arange(2 * tq, dtype=jnp.int32)[:, None]
    qq = jnp.arange(tq, dtype=jnp.int32)[None, :]
    span = 3 * tq
    rel = jnp.arange(span, dtype=jnp.int32) - (2 * tq - 1)
    vec = rel_bias[_rel_bucket(rel)].astype(F32) * LOG2E
    m = jnp.tile(vec, (tq, 1))[:tq * (span - 1)].reshape(tq, span - 1, -1)
    bias = m[:, tq - 1:, :].transpose(1, 0, 2)
    allowed = (kk < tq) | (((kk - tq) // CHUNK) <= (qq // CHUNK))
    bias = jnp.where(allowed[..., None], bias, NEG)
    near = bias.reshape(2, tq, tq, DIFF_HEADS, 2).transpose(3, 0, 1, 4, 2)
    near = near.reshape(DIFF_HEADS, 2, tq, 2 * tq)
    far = rel_bias[_rel_bucket(jnp.asarray(-2 * REL_MAX_DIST, jnp.int32))].astype(F32) * LOG2E
    far = jnp.repeat(far.reshape(DIFF_HEADS, 2), tq, axis=1)
    far = jnp.broadcast_to(far[:, None, None, :], (DIFF_HEADS, 1, tq, 2 * tq))
    return jnp.concatenate([far, near], axis=1)


def _pos_tables(seq):
    pos = jnp.arange(seq, dtype=F32)
    inv_freq = ROPE_BASE ** (-jnp.arange(0, MLA_ROPE, 2, dtype=F32) / MLA_ROPE)
    ang = pos[:, None] * inv_freq[None, :]
    cos, sin = jnp.cos(ang), jnp.sin(ang)
    pad = jnp.zeros((seq, LANE - MLA_ROPE), F32)
    n_chunks = seq // CHUNK
    assert n_chunks <= LANE - MLA_ROPE
    chunk = (jnp.arange(seq, dtype=jnp.int32) // CHUNK)[:, None]
    cid = jnp.arange(LANE, dtype=jnp.int32)[None, :] - MLA_ROPE
    valid = (cid >= 0) & (cid < n_chunks)
    mq = jnp.where(valid & (cid > chunk), NEG, 0.0).astype(F32)
    mk = jnp.where(valid & (cid == chunk), 1.0, 0.0).astype(F32)
    return (jnp.concatenate([cos, cos, pad], axis=1),
            jnp.concatenate([-sin, sin, pad], axis=1), mq, mk)


def _swap_halves(w):
    half = w.shape[-1] // 2
    return jnp.concatenate([w[..., half:], w[..., :half]], axis=-1)


def _pad_lanes(w):
    return jnp.pad(w, [(0, 0)] * (w.ndim - 1) + [(0, LANE - w.shape[-1])])


def kernel(x, norm_g, w_in, diff_lambda, diff_subln_g, mla_q_norm_g, mla_w_q_b,
           mla_kv_norm_g, mla_w_kv_b, w_out, rel_bias, final_norm_g):
    b, s, d = x.shape
    assert d == D_MODEL and s % TQ_MLA == 0 and s >= 2 * TQ_MLA and (b * s) % TM_PROJ == 0
    m = b * s
    pos_tables = _pos_tables(s)
    tab = _bias_tables(rel_bias)
    colscale = jnp.concatenate([
        jnp.full((1, DIFF_WIDTH), DIFF_HEAD_DIM ** -0.5 * LOG2E, F32),
        jnp.ones((1, QKV_WIDTH - DIFF_WIDTH), F32)], axis=1)
    gf = final_norm_g.reshape(1, D_MODEL)

    x2 = x.reshape(m, d)
    for l in range(DEPTH):
        wl = w_in[l]
        w_qkv = wl[:, :QKV_WIDTH].astype(BF16)
        w_gate = wl[:, GATE_OFF:].astype(BF16)
        w_kr = wl[:, GATE_OFF - MLA_ROPE:GATE_OFF]
        w_lat = jnp.concatenate([wl[:, LAT_OFF:GATE_OFF - MLA_ROPE], _pad_lanes(w_kr),
                                 _pad_lanes(_swap_halves(w_kr))], axis=1).astype(BF16)
        wq = mla_w_q_b[l].reshape(MLA_Q_LORA, MLA_HEADS, MLA_NOPE + MLA_ROPE)
        wq_rope = wq[:, :, MLA_NOPE:]
        w_q2 = jnp.concatenate([
            wq[:, :, :MLA_NOPE].reshape(MLA_Q_LORA, -1),
            _pad_lanes(wq_rope).reshape(MLA_Q_LORA, -1),
            _pad_lanes(_swap_halves(wq_rope)).reshape(MLA_Q_LORA, -1)], axis=1).astype(BF16)
        w_kv2 = mla_w_kv_b[l].astype(BF16)
        g = norm_g[l].reshape(1, D_MODEL)

        lam_init = 0.8 - 0.6 * math.exp(-0.3 * l)
        lp = diff_lambda[l].astype(F32)
        lam = jnp.exp(jnp.sum(lp[0] * lp[1])) - jnp.exp(jnp.sum(lp[2] * lp[3])) + lam_init

        qkv, h = _qkv_proj(x2, g, w_qkv, colscale)
        gates3 = _gate_proj(h, w_gate).reshape(b, s, MIX_WIDTH)
        qn, qr, kv, kr = _latent(h, w_lat, mla_q_norm_g[l].reshape(1, -1),
                                 mla_kv_norm_g[l].reshape(1, -1), w_q2, w_kv2, pos_tables, s)
        ya = _diff_attn(qkv.reshape(b, s, QKV_WIDTH), gates3, lam.reshape(1, 1), tab,
                        diff_subln_g[l].reshape(1, -1), lam_init)
        yb = _mla_attn(qn.reshape(b, s, -1), qr.reshape(b, s, -1), kv.reshape(b, s, -1),
                       kr.reshape(b, s, -1), gates3)
        x2 = _out_proj(x2, ya.reshape(m, -1), yb.reshape(m, -1), w_out[l].astype(BF16), gf,
                       final=(l == DEPTH - 1))
    return x2.reshape(b, s, d)
```

```python
import functools
import math

import jax
import jax.numpy as jnp
from jax import lax
from jax.experimental import pallas as pl
from jax.experimental.pallas import tpu as pltpu

D_MODEL = 2048
DEPTH = 2
CHUNK = 64
MIX_WIDTH = D_MODEL
DIFF_WIDTH = MIX_WIDTH // 2
MLA_WIDTH = MIX_WIDTH - DIFF_WIDTH
DIFF_HEAD_DIM = 64
DIFF_HEADS = DIFF_WIDTH // (2 * DIFF_HEAD_DIM)
MLA_V_DIM = 128
MLA_HEADS = MLA_WIDTH // MLA_V_DIM
MLA_NOPE = 128
MLA_ROPE = 64
MLA_Q_LORA = 512
MLA_KV_LORA = 256
ROPE_BASE = 10000.0
REL_BUCKETS = 32
REL_MAX_DIST = 128
EPS = 1e-6
NEG = -1e30
QKV_WIDTH = 3 * DIFF_WIDTH
LAT_OFF = QKV_WIDTH
GATE_OFF = LAT_OFF + MLA_Q_LORA + MLA_KV_LORA + MLA_ROPE

LOG2E = math.log2(math.e)
LANE = 128
VMEM_LIMIT = 56 * 1024 * 1024

BF16 = jnp.bfloat16
F32 = jnp.float32

TM_PROJ = 1024
TN_PROJ = 1024
RC_NORM = 256
TM_LAT = 512
TM_OUT = 512
TQ_DIFF = 256
TQ_MLA = 512
NH_DIFF = 1
NH_MLA = 1
TRIPS_DIFF_FAR = ((9, 12),)
TRIPS_DIFF_NEAR = ((2, 16),)
TRIPS_MLA = ((3, 12),)
QK_LEAD = 2
S_SLOTS = 4

DV_AUG = MLA_V_DIM + 16


def _rms_scale(xf):
    return lax.rsqrt(jnp.mean(xf * xf, axis=-1, keepdims=True) + EPS)


def _qkv_proj_kernel(x_ref, g_ref, w_ref, cs_ref, o_ref, h_ref):
    @pl.when(pl.program_id(1) == 0)
    def _():
        def body(c, carry):
            rows = pl.ds(pl.multiple_of(c * RC_NORM, RC_NORM), RC_NORM)
            xf = x_ref[rows, :]
            h_ref[rows, :] = (xf * _rms_scale(xf) * g_ref[...]).astype(BF16)
            return carry
        lax.fori_loop(0, TM_PROJ // RC_NORM, body, 0)

    acc = jnp.dot(h_ref[...], w_ref[...], preferred_element_type=F32)
    o_ref[...] = (acc * cs_ref[...]).astype(BF16)


def _qkv_proj(x2, g, w_qkv, colscale):
    m = x2.shape[0]
    n = w_qkv.shape[1]
    return pl.pallas_call(
        _qkv_proj_kernel,
        grid=(m // TM_PROJ, n // TN_PROJ),
        in_specs=[
            pl.BlockSpec((TM_PROJ, D_MODEL), lambda i, j: (i, 0)),
            pl.BlockSpec((1, D_MODEL), lambda i, j: (0, 0)),
            pl.BlockSpec((D_MODEL, TN_PROJ), lambda i, j: (0, j)),
            pl.BlockSpec((1, TN_PROJ), lambda i, j: (0, j)),
        ],
        out_specs=[pl.BlockSpec((TM_PROJ, TN_PROJ), lambda i, j: (i, j)),
                   pl.BlockSpec((TM_PROJ, D_MODEL), lambda i, j: (i, 0))],
        out_shape=[jax.ShapeDtypeStruct((m, n), BF16),
                   jax.ShapeDtypeStruct((m, D_MODEL), BF16)],
        compiler_params=pltpu.CompilerParams(
            dimension_semantics=("parallel", "arbitrary"), vmem_limit_bytes=VMEM_LIMIT),
        name="qkv_proj",
    )(x2, g, w_qkv, colscale)


def _gate_proj_kernel(h_ref, w_ref, o_ref):
    acc = jnp.dot(h_ref[...], w_ref[...], preferred_element_type=F32)
    o_ref[...] = (acc * jax.nn.sigmoid(acc)).astype(BF16)


def _gate_proj(h, w_gate):
    m = h.shape[0]
    n = w_gate.shape[1]
    return pl.pallas_call(
        _gate_proj_kernel,
        grid=(m // TM_PROJ, n // TN_PROJ),
        in_specs=[
            pl.BlockSpec((TM_PROJ, D_MODEL), lambda i, j: (i, 0)),
            pl.BlockSpec((D_MODEL, TN_PROJ), lambda i, j: (0, j)),
        ],
        out_specs=pl.BlockSpec((TM_PROJ, TN_PROJ), lambda i, j: (i, j)),
        out_shape=jax.ShapeDtypeStruct((m, n), BF16),
        compiler_params=pltpu.CompilerParams(
            dimension_semantics=("parallel", "arbitrary"), vmem_limit_bytes=VMEM_LIMIT),
        name="gate_proj",
    )(h, w_gate)


def _latent_kernel(h_ref, wl_ref, gq_ref, gkv_ref, wq_ref, wkv_ref, cos_ref, sin_ref,
                   mq_ref, mk_ref, qn_ref, qr_ref, kv_ref, kr_ref, *, q_scale):
    lat = jnp.dot(h_ref[...], wl_ref[...], preferred_element_type=F32)
    cq = lat[:, :MLA_Q_LORA]
    ckv = lat[:, MLA_Q_LORA:MLA_Q_LORA + MLA_KV_LORA]
    ka = lat[:, MLA_Q_LORA + MLA_KV_LORA:MLA_Q_LORA + MLA_KV_LORA + LANE]
    kb = lat[:, MLA_Q_LORA + MLA_KV_LORA + LANE:]
    cos = cos_ref[...]
    sin = sin_ref[...]
    mq = mq_ref[...]

    cqn = (cq * _rms_scale(cq) * gq_ref[...]).astype(BF16)
    qall = jnp.dot(cqn, wq_ref[...], preferred_element_type=F32) * q_scale
    nope_w = MLA_HEADS * MLA_NOPE
    qn_ref[...] = qall[:, :nope_w].astype(BF16)
    for hh in range(MLA_HEADS):
        a = qall[:, nope_w + hh * LANE:nope_w + (hh + 1) * LANE]
        b = qall[:, 2 * nope_w + hh * LANE:2 * nope_w + (hh + 1) * LANE]
        qr_ref[:, hh * LANE:(hh + 1) * LANE] = (a * cos + b * sin + mq).astype(BF16)

    ckvn = (ckv * _rms_scale(ckv) * gkv_ref[...]).astype(BF16)
    kv_ref[...] = jnp.dot(ckvn, wkv_ref[...], preferred_element_type=F32).astype(BF16)
    kr_ref[...] = (ka * cos + kb * sin + mk_ref[...]).astype(BF16)


def _latent(h, w_lat, gq, gkv, w_q2, w_kv2, pos_tables, seq):
    m = h.shape[0]
    seq_tiles = seq // TM_LAT
    const = lambda i: (0, 0)
    row = lambda i: (i, 0)
    pos = lambda i: (i % seq_tiles, 0)
    q_scale = (MLA_NOPE + MLA_ROPE) ** -0.5 * LOG2E
    return pl.pallas_call(
        functools.partial(_latent_kernel, q_scale=q_scale),
        grid=(m // TM_LAT,),
        in_specs=[
            pl.BlockSpec((TM_LAT, D_MODEL), row),
            pl.BlockSpec(w_lat.shape, const),
            pl.BlockSpec((1, MLA_Q_LORA), const),
            pl.BlockSpec((1, MLA_KV_LORA), const),
            pl.BlockSpec(w_q2.shape, const),
            pl.BlockSpec(w_kv2.shape, const),
        ] + [pl.BlockSpec((TM_LAT, LANE), pos)] * len(pos_tables),
        out_specs=[
            pl.BlockSpec((TM_LAT, MLA_HEADS * LANE), row),
            pl.BlockSpec((TM_LAT, MLA_HEADS * LANE), row),
            pl.BlockSpec((TM_LAT, w_kv2.shape[1]), row),
            pl.BlockSpec((TM_LAT, LANE), row),
        ],
        out_shape=[
            jax.ShapeDtypeStruct((m, MLA_HEADS * LANE), BF16),
            jax.ShapeDtypeStruct((m, MLA_HEADS * LANE), BF16),
            jax.ShapeDtypeStruct((m, w_kv2.shape[1]), BF16),
            jax.ShapeDtypeStruct((m, LANE), BF16),
        ],
        compiler_params=pltpu.CompilerParams(
            dimension_semantics=("parallel",), vmem_limit_bytes=VMEM_LIMIT),
        name="latent_proj",
    )(h, w_lat, gq, gkv, w_q2, w_kv2, *pos_tables)


def _attn_pipeline(first, advance, is_first, is_last, trip_plan, heads, qk_operands, vt_tile,
                   bias_tile, col_bias, finalize, s_buf, mt_buf, m_ref, acc_ref):
    def qk_stage(q, t, slot):
        for hh in heads:
            kt, qt = qk_operands(hh, q, t)
            s = jnp.dot(kt, qt, preferred_element_type=F32)
            bias = bias_tile(hh, q, t)
            if bias is not None:
                s = s + bias
            s_buf[hh, slot] = s
            mt_buf[hh, slot] = jnp.max(s, axis=0, keepdims=True)

    def sm_pv_stage(q, t, s_slot):
        for hh in heads:
            s = s_buf[hh, s_slot]
            mt = mt_buf[hh, s_slot]
            cb = col_bias(hh)
            if cb is not None:
                mt = mt + cb
            m_old = jnp.where(is_first(q, t), NEG, m_ref[hh, q])
            m_new = jnp.maximum(m_old, mt)
            m_ref[hh, q] = m_new
            shift = m_new if cb is None else m_new - cb
            p = jnp.exp2(s - shift).astype(BF16)
            pv = jnp.dot(vt_tile(hh, t), p, preferred_element_type=F32)
            acc_ref[hh, q] = jnp.exp2(m_old - m_new) * acc_ref[hh, q] + pv

    ahead = [first]
    for _ in range(QK_LEAD):
        ahead.append(advance(*ahead[-1]))
    for i in range(QK_LEAD):
        qk_stage(*ahead[i], i)

    def step(carry, j):
        lead, cur = carry[:QK_LEAD], carry[QK_LEAD]
        qk_stage(*lead[0], (j + QK_LEAD) % S_SLOTS)
        sm_pv_stage(*cur, j % S_SLOTS)
        return (advance(*lead[0]),) + lead, (is_last and is_last(*cur), cur[0])

    assert all(steps % S_SLOTS == 0 for _, steps in trip_plan)

    def make_body(steps):
        def body(_, carry):
            done = []
            for j in range(steps):
                carry, d = step(carry, j)
                done.append(d)
            for ended, q in done:
                if ended is None:
                    continue

                @pl.when(ended)
                def _():
                    for hh in heads:
                        finalize(hh, q, acc_ref[hh, q])
            return carry
        return body

    carry = tuple(reversed(ahead))
    for n_trips, steps in trip_plan:
        carry = lax.fori_loop(0, n_trips, make_body(steps), carry)


def _init_attn_state(m_ref, acc_ref):
    m_ref[...] = jnp.full(m_ref.shape, NEG, F32)
    acc_ref[...] = jnp.zeros(acc_ref.shape, F32)


def _attn_scratch(nh, n_kv, n_slots, tk, n):
    return [pltpu.VMEM((nh, n_kv, DV_AUG, tk), BF16),
            pltpu.VMEM((nh, S_SLOTS, tk, n), F32),
            pltpu.VMEM((nh, S_SLOTS, 1, n), F32),
            pltpu.VMEM((nh, n_slots, 1, n), F32),
            pltpu.VMEM((nh, n_slots, DV_AUG, n), F32)]


def _fill_vt(vt_buf, hh, t, v):
    dv, tk = v.shape[1], v.shape[0]
    vt_buf[hh, t, :dv, :] = v.T
    row = lax.broadcasted_iota(jnp.int32, (DV_AUG - dv, tk), 0)
    vt_buf[hh, t, dv:, :] = jnp.where(row == 0, 1.0, 0.0).astype(BF16)


def _diff_attn_kernel(lam_ref, q_ref, k_ref, v_ref, gate_ref, tab_ref, cf_ref, g_ref, o_ref,
                      qz_buf, vt_buf, *scratch, out_scale):
    tq = TQ_DIFF
    hd = 2 * DIFF_HEAD_DIM
    assert hd == MLA_V_DIM
    n_q = q_ref.shape[1] // tq
    heads = range(NH_DIFF)
    lam = lam_ref[0, 0]

    def rows_of(i):
        return pl.ds(pl.multiple_of(i * tq, tq), tq)

    def cols_of(hh):
        return slice(hh * hd, (hh + 1) * hd)

    feat = lax.broadcasted_iota(jnp.int32, (hd, tq), 0)

    def prep(qi, carry):
        for hh in heads:
            qb = q_ref[0, rows_of(qi), cols_of(hh)].T
            zero = jnp.zeros_like(qb)
            qz_buf[hh, qi, :, :tq] = jnp.where(feat < DIFF_HEAD_DIM, qb, zero)
            qz_buf[hh, qi, :, tq:] = jnp.where(feat >= DIFF_HEAD_DIM, qb, zero)
            _fill_vt(vt_buf, hh, qi, v_ref[0, rows_of(qi), cols_of(hh)])
        return carry
    lax.fori_loop(0, n_q, prep, 0)
    idle = n_q
    for hh in heads:
        qz_buf[hh, idle] = jnp.zeros(qz_buf.shape[2:], BF16)

    def qk_operands(hh, qi, t):
        return k_ref[0, rows_of(t), cols_of(hh)], qz_buf[hh, qi]

    def vt_tile(hh, t):
        return vt_buf[hh, t]

    def far_advance(q, t):
        nxt = t >= q - 2
        q2 = jnp.where(nxt, q + 1, q)
        return jnp.where(q2 >= n_q, idle, q2), jnp.where(nxt, 0, t + 1)

    def near_advance(q, t):
        nxt = t >= q
        q2 = jnp.where(nxt, q + 1, q)
        return (jnp.where(q2 >= n_q, idle, q2),
                jnp.where(q2 >= n_q, 0, jnp.where(nxt, q, t + 1)))

    def near_bias(hh, qi, t):
        return tab_ref[hh, jnp.maximum(t - qi + 1, 0)]

    def finalize(hh, qi, acc):
        inv = 1.0 / acc[hd:hd + 1]
        o_t = acc[:hd, :tq] * inv[:, :tq] - lam * (acc[:hd, tq:] * inv[:, tq:])
        o_t = o_t * lax.rsqrt(jnp.mean(o_t * o_t, axis=0, keepdims=True) + EPS)
        o = o_t.T * g_ref[...] * out_scale
        gate = gate_ref[0, rows_of(qi), cols_of(hh)].astype(F32)
        o_ref[0, rows_of(qi), cols_of(hh)] = (o * gate).astype(BF16)

    assert sum(n * steps for n, steps in TRIPS_DIFF_FAR) >= (n_q - 2) * (n_q - 1) // 2
    assert sum(n * steps for n, steps in TRIPS_DIFF_NEAR) >= 2 * n_q - 1
    _init_attn_state(*scratch[2:])
    two = jnp.int32(2)
    zero = jnp.int32(0)
    _attn_pipeline((two, zero), far_advance, lambda q, t: t == 0, None,
                   TRIPS_DIFF_FAR, heads, qk_operands, vt_tile, lambda hh, qi, t: None,
                   lambda hh: cf_ref[hh], finalize, *scratch)
    _attn_pipeline((zero, zero), near_advance, lambda q, t: q + t < 2,
                   lambda q, t: jnp.logical_and(t == q, q < n_q), TRIPS_DIFF_NEAR, heads,
                   qk_operands, vt_tile, near_bias, lambda hh: None, finalize, *scratch)


def _diff_attn(qkv3, gates3, lam, tab, cf, g, lam_init):
    b, s, _ = qkv3.shape
    tq = TQ_DIFF
    hd = 2 * DIFF_HEAD_DIM
    nh = NH_DIFF
    blk = nh * hd
    kcol, vcol = DIFF_WIDTH // blk, 2 * DIFF_WIDTH // blk
    seq_blk = lambda off: pl.BlockSpec((1, s, blk), lambda bi, h: (bi, 0, off + h))
    return pl.pallas_call(
        functools.partial(_diff_attn_kernel, out_scale=1.0 - lam_init),
        grid=(b, DIFF_HEADS // nh),
        in_specs=[
            pl.BlockSpec(memory_space=pltpu.SMEM),
            seq_blk(0), seq_blk(kcol), seq_blk(vcol), seq_blk(0),
            pl.BlockSpec((nh, 2, tq, 2 * tq), lambda bi, h: (h, 0, 0, 0)),
            pl.BlockSpec((nh, 1, 2 * tq), lambda bi, h: (h, 0, 0)),
            pl.BlockSpec((1, hd), lambda bi, h: (0, 0)),
        ],
        out_specs=seq_blk(0),
        out_shape=jax.ShapeDtypeStruct((b, s, DIFF_WIDTH), BF16),
        scratch_shapes=[pltpu.VMEM((nh, s // tq + 1, hd, 2 * tq), BF16)]
        + _attn_scratch(nh, s // tq, s // tq + 1, tq, 2 * tq),
        compiler_params=pltpu.CompilerParams(
            dimension_semantics=("parallel", "arbitrary"), vmem_limit_bytes=VMEM_LIMIT),
        name="diff_attn",
    )(lam, qkv3, qkv3, qkv3, gates3, tab, cf, g)


def _mla_attn_kernel(qn_ref, qr_ref, kv_ref, kr_ref, gate_ref, o_ref, qt_buf, vt_buf, *scratch):
    tq = TQ_MLA
    n_q = qn_ref.shape[1] // tq
    heads = range(NH_MLA)

    def rows_of(i):
        return pl.ds(pl.multiple_of(i * tq, tq), tq)

    def cols_of(hh):
        return slice(hh * LANE, (hh + 1) * LANE)

    def prep(qi, carry):
        for hh in heads:
            qt_buf[hh, qi, :LANE, :] = qn_ref[0, rows_of(qi), cols_of(hh)].T
            qt_buf[hh, qi, LANE:, :] = qr_ref[0, rows_of(qi), cols_of(hh)].T
            _fill_vt(vt_buf, hh, qi, kv_ref[0, rows_of(qi), cols_of(2 * hh + 1)])
        return carry
    lax.fori_loop(0, n_q, prep, 0)

    def qk_operands(hh, qi, t):
        kt = jnp.concatenate([kv_ref[0, rows_of(t), cols_of(2 * hh)], kr_ref[0, rows_of(t), :]],
                             axis=1)
        return kt, qt_buf[hh, qi]

    def vt_tile(hh, t):
        return vt_buf[hh, t]

    def finalize(hh, qi, acc):
        dv = MLA_V_DIM
        o = (acc[:dv] * (1.0 / acc[dv:dv + 1])).T
        gate = gate_ref[0, rows_of(qi), cols_of(hh)].astype(F32)
        o_ref[0, rows_of(qi), cols_of(hh)] = (o * gate).astype(BF16)

    last = n_q - 1

    def advance(q, t):
        end = t >= q
        return (jnp.where(end, jnp.minimum(q + 1, last), q),
                jnp.where(end, jnp.where(q >= last, t, 0), t + 1))

    assert sum(n * steps for n, steps in TRIPS_MLA) == n_q * (n_q + 1) // 2
    _init_attn_state(*scratch[2:])
    zero = jnp.int32(0)
    _attn_pipeline((zero, zero), advance, lambda q, t: t == 0, lambda q, t: t == q, TRIPS_MLA,
                   heads, qk_operands, vt_tile, lambda hh, qi, t: None, lambda hh: None,
                   finalize, *scratch)


def _mla_attn(qn3, qr3, kv3, kr3, gates3):
    b, s, _ = qn3.shape
    tq = TQ_MLA
    nh = NH_MLA
    blk = nh * LANE
    gcol = DIFF_WIDTH // blk
    seq_blk = lambda w, off: pl.BlockSpec((1, s, w), lambda bi, h: (bi, 0, off + h))
    return pl.pallas_call(
        _mla_attn_kernel,
        grid=(b, MLA_HEADS // nh),
        in_specs=[
            seq_blk(blk, 0), seq_blk(blk, 0), seq_blk(2 * blk, 0),
            pl.BlockSpec((1, s, LANE), lambda bi, h: (bi, 0, 0)),
            seq_blk(blk, gcol),
        ],
        out_specs=seq_blk(blk, 0),
        out_shape=jax.ShapeDtypeStruct((b, s, MLA_WIDTH), BF16),
        scratch_shapes=[pltpu.VMEM((nh, s // tq, 2 * LANE, tq), BF16)]
        + _attn_scratch(nh, s // tq, s // tq, tq, tq),
        compiler_params=pltpu.CompilerParams(
            dimension_semantics=("parallel", "arbitrary"), vmem_limit_bytes=VMEM_LIMIT),
        name="mla_attn",
    )(qn3, qr3, kv3, kr3, gates3)


def _out_proj_kernel(x_ref, ya_ref, yb_ref, w_ref, gf_ref, o_ref, *, final):
    acc = jnp.dot(ya_ref[...], w_ref[:DIFF_WIDTH, :], preferred_element_type=F32)
    acc = acc + jnp.dot(yb_ref[...], w_ref[DIFF_WIDTH:, :], preferred_element_type=F32)
    xn = x_ref[...] + acc
    if final:
        xn = xn * _rms_scale(xn) * gf_ref[...]
    o_ref[...] = xn


def _out_proj(x2, ya, yb, w_out, gf, final):
    m = x2.shape[0]
    row = lambda i: (i, 0)
    const = lambda i: (0, 0)
    return pl.pallas_call(
        functools.partial(_out_proj_kernel, final=final),
        grid=(m // TM_OUT,),
        in_specs=[
            pl.BlockSpec((TM_OUT, D_MODEL), row),
            pl.BlockSpec((TM_OUT, DIFF_WIDTH), row),
            pl.BlockSpec((TM_OUT, MLA_WIDTH), row),
            pl.BlockSpec((MIX_WIDTH, D_MODEL), const),
            pl.BlockSpec((1, D_MODEL), const),
        ],
        out_specs=pl.BlockSpec((TM_OUT, D_MODEL), row),
        out_shape=jax.ShapeDtypeStruct((m, D_MODEL), F32),
        compiler_params=pltpu.CompilerParams(
            dimension_semantics=("parallel",), vmem_limit_bytes=VMEM_LIMIT),
        name="out_proj",
    )(x2, ya, yb, w_out, gf)


def _rel_bucket(rel):
    nb = REL_BUCKETS // 2
    max_exact = nb // 2
    ret = (rel > 0).astype(jnp.int32) * nb
    n = jnp.abs(rel)
    nf = jnp.maximum(n, 1).astype(F32)
    large = max_exact + (jnp.log(nf / max_exact) / math.log(REL_MAX_DIST / max_exact)
                         * (nb - max_exact)).astype(jnp.int32)
    large = jnp.minimum(large, nb - 1)
    return ret + jnp.where(n < max_exact, n, large)


def _bias_tables(rel_bias):
    tq = TQ_DIFF
    kk = jnp.arange(2 * tq, dtype=jnp.int32)[:, None]
    qq = jnp.arange(tq, dtype=jnp.int32)[None, :]
    span = 3 * tq
    rel = jnp.arange(span, dtype=jnp.int32) - (2 * tq - 1)
    vec = rel_bias[_rel_bucket(rel)].astype(F32) * LOG2E
    m = jnp.tile(vec, (tq, 1))[:tq * (span - 1)].reshape(tq, span - 1, -1)
    bias = m[:, tq - 1:, :].transpose(1, 0, 2)
    allowed = (kk < tq) | (((kk - tq) // CHUNK) <= (qq // CHUNK))
    bias = jnp.where(allowed[..., None], bias, NEG)
    near = bias.reshape(2, tq, tq, DIFF_HEADS, 2).transpose(3, 0, 1, 4, 2)
    near = near.reshape(DIFF_HEADS, 2, tq, 2 * tq)
    far = rel_bias[_rel_bucket(jnp.asarray(-2 * REL_MAX_DIST, jnp.int32))].astype(F32) * LOG2E
    far = jnp.repeat(far.reshape(DIFF_HEADS, 2), tq, axis=1)
    return near, far.reshape(DIFF_HEADS, 1, 2 * tq)


def _pos_tables(seq):
    pos = jnp.arange(seq, dtype=F32)
    inv_freq = ROPE_BASE ** (-jnp.arange(0, MLA_ROPE, 2, dtype=F32) / MLA_ROPE)
    ang = pos[:, None] * inv_freq[None, :]
    cos, sin = jnp.cos(ang), jnp.sin(ang)
    pad = jnp.zeros((seq, LANE - MLA_ROPE), F32)
    n_chunks = seq // CHUNK
    assert n_chunks <= LANE - MLA_ROPE
    chunk = (jnp.arange(seq, dtype=jnp.int32) // CHUNK)[:, None]
    cid = jnp.arange(LANE, dtype=jnp.int32)[None, :] - MLA_ROPE
    valid = (cid >= 0) & (cid < n_chunks)
    mq = jnp.where(valid & (cid > chunk), NEG, 0.0).astype(F32)
    mk = jnp.where(valid & (cid == chunk), 1.0, 0.0).astype(F32)
    return (jnp.concatenate([cos, cos, pad], axis=1),
            jnp.concatenate([-sin, sin, pad], axis=1), mq, mk)


def _swap_halves(w):
    half = w.shape[-1] // 2
    return jnp.concatenate([w[..., half:], w[..., :half]], axis=-1)


def _pad_lanes(w):
    return jnp.pad(w, [(0, 0)] * (w.ndim - 1) + [(0, LANE - w.shape[-1])])


def kernel(x, norm_g, w_in, diff_lambda, diff_subln_g, mla_q_norm_g, mla_w_q_b,
           mla_kv_norm_g, mla_w_kv_b, w_out, rel_bias, final_norm_g):
    b, s, d = x.shape
    assert d == D_MODEL and s % TQ_MLA == 0 and s >= 2 * TQ_MLA and (b * s) % TM_PROJ == 0
    m = b * s
    pos_tables = _pos_tables(s)
    tab, cf = _bias_tables(rel_bias)
    colscale = jnp.concatenate([
        jnp.full((1, DIFF_WIDTH), DIFF_HEAD_DIM ** -0.5 * LOG2E, F32),
        jnp.ones((1, QKV_WIDTH - DIFF_WIDTH), F32)], axis=1)
    gf = final_norm_g.reshape(1, D_MODEL)

    x2 = x.reshape(m, d)
    for l in range(DEPTH):
        wl = w_in[l]
        w_qkv = wl[:, :QKV_WIDTH].astype(BF16)
        w_gate = wl[:, GATE_OFF:].astype(BF16)
        w_kr = wl[:, GATE_OFF - MLA_ROPE:GATE_OFF]
        w_lat = jnp.concatenate([wl[:, LAT_OFF:GATE_OFF - MLA_ROPE], _pad_lanes(w_kr),
                                 _pad_lanes(_swap_halves(w_kr))], axis=1).astype(BF16)
        wq = mla_w_q_b[l].reshape(MLA_Q_LORA, MLA_HEADS, MLA_NOPE + MLA_ROPE)
        wq_rope = wq[:, :, MLA_NOPE:]
        w_q2 = jnp.concatenate([
            wq[:, :, :MLA_NOPE].reshape(MLA_Q_LORA, -1),
            _pad_lanes(wq_rope).reshape(MLA_Q_LORA, -1),
            _pad_lanes(_swap_halves(wq_rope)).reshape(MLA_Q_LORA, -1)], axis=1).astype(BF16)
        w_kv2 = mla_w_kv_b[l].astype(BF16)
        g = norm_g[l].reshape(1, D_MODEL)

        lam_init = 0.8 - 0.6 * math.exp(-0.3 * l)
        lp = diff_lambda[l].astype(F32)
        lam = jnp.exp(jnp.sum(lp[0] * lp[1])) - jnp.exp(jnp.sum(lp[2] * lp[3])) + lam_init

        qkv, h = _qkv_proj(x2, g, w_qkv, colscale)
        gates3 = _gate_proj(h, w_gate).reshape(b, s, MIX_WIDTH)
        qn, qr, kv, kr = _latent(h, w_lat, mla_q_norm_g[l].reshape(1, -1),
                                 mla_kv_norm_g[l].reshape(1, -1), w_q2, w_kv2, pos_tables, s)
        ya = _diff_attn(qkv.reshape(b, s, QKV_WIDTH), gates3, lam.reshape(1, 1), tab, cf,
                        diff_subln_g[l].reshape(1, -1), lam_init)
        yb = _mla_attn(qn.reshape(b, s, -1), qr.reshape(b, s, -1), kv.reshape(b, s, -1),
                       kr.reshape(b, s, -1), gates3)
        x2 = _out_proj(x2, ya.reshape(m, -1), yb.reshape(m, -1), w_out[l].astype(BF16), gf,
                       final=(l == DEPTH - 1))
    return x2.reshape(b, s, d)
```

```python
import functools
import math

import jax
import jax.numpy as jnp
from jax import lax
from jax.experimental import pallas as pl
from jax.experimental.pallas import tpu as pltpu

D_MODEL = 2048
DEPTH = 2
CHUNK = 64
MIX_WIDTH = D_MODEL
DIFF_WIDTH = MIX_WIDTH // 2
MLA_WIDTH = MIX_WIDTH - DIFF_WIDTH
DIFF_HEAD_DIM = 64
DIFF_HEADS = DIFF_WIDTH // (2 * DIFF_HEAD_DIM)
MLA_V_DIM = 128
MLA_HEADS = MLA_WIDTH // MLA_V_DIM
MLA_NOPE = 128
MLA_ROPE = 64
MLA_Q_LORA = 512
MLA_KV_LORA = 256
ROPE_BASE = 10000.0
REL_BUCKETS = 32
REL_MAX_DIST = 128
EPS = 1e-6
NEG = -1e30
QKV_WIDTH = 3 * DIFF_WIDTH
LAT_OFF = QKV_WIDTH
GATE_OFF = LAT_OFF + MLA_Q_LORA + MLA_KV_LORA + MLA_ROPE

LOG2E = math.log2(math.e)
LANE = 128
VMEM_LIMIT = 56 * 1024 * 1024

BF16 = jnp.bfloat16
F32 = jnp.float32

TM_PROJ = 1024
TN_PROJ = 1024
RC_NORM = 256
TM_LAT = 512
TM_OUT = 512
TQ_DIFF = 256
TQ_MLA = 512
NH_DIFF = 1
NH_MLA = 1
TRIPS_DIFF_FAR = ((9, 12),)
TRIPS_DIFF_NEAR = ((2, 16),)
TRIPS_MLA_FAR = ((2, 12), (1, 4))
TRIPS_MLA_DIAG = ((2, 4),)
QK_LEAD = 2
S_SLOTS = 4

DV_AUG = MLA_V_DIM + 16


def _rms_scale(xf):
    return lax.rsqrt(jnp.mean(xf * xf, axis=-1, keepdims=True) + EPS)


def _qkv_proj_kernel(x_ref, g_ref, w_ref, cs_ref, o_ref, h_ref):
    @pl.when(pl.program_id(1) == 0)
    def _():
        def body(c, carry):
            rows = pl.ds(pl.multiple_of(c * RC_NORM, RC_NORM), RC_NORM)
            xf = x_ref[rows, :]
            h_ref[rows, :] = (xf * _rms_scale(xf) * g_ref[...]).astype(BF16)
            return carry
        lax.fori_loop(0, TM_PROJ // RC_NORM, body, 0)

    acc = jnp.dot(h_ref[...], w_ref[...], preferred_element_type=F32)
    o_ref[...] = (acc * cs_ref[...]).astype(BF16)


def _qkv_proj(x2, g, w_qkv, colscale):
    m = x2.shape[0]
    n = w_qkv.shape[1]
    return pl.pallas_call(
        _qkv_proj_kernel,
        grid=(m // TM_PROJ, n // TN_PROJ),
        in_specs=[
            pl.BlockSpec((TM_PROJ, D_MODEL), lambda i, j: (i, 0)),
            pl.BlockSpec((1, D_MODEL), lambda i, j: (0, 0)),
            pl.BlockSpec((D_MODEL, TN_PROJ), lambda i, j: (0, j)),
            pl.BlockSpec((1, TN_PROJ), lambda i, j: (0, j)),
        ],
        out_specs=[pl.BlockSpec((TM_PROJ, TN_PROJ), lambda i, j: (i, j)),
                   pl.BlockSpec((TM_PROJ, D_MODEL), lambda i, j: (i, 0))],
        out_shape=[jax.ShapeDtypeStruct((m, n), BF16),
                   jax.ShapeDtypeStruct((m, D_MODEL), BF16)],
        compiler_params=pltpu.CompilerParams(
            dimension_semantics=("parallel", "arbitrary"), vmem_limit_bytes=VMEM_LIMIT),
        name="qkv_proj",
    )(x2, g, w_qkv, colscale)


def _gate_proj_kernel(h_ref, w_ref, o_ref):
    acc = jnp.dot(h_ref[...], w_ref[...], preferred_element_type=F32)
    o_ref[...] = (acc * jax.nn.sigmoid(acc)).astype(BF16)


def _gate_proj(h, w_gate):
    m = h.shape[0]
    n = w_gate.shape[1]
    return pl.pallas_call(
        _gate_proj_kernel,
        grid=(m // TM_PROJ, n // TN_PROJ),
        in_specs=[
            pl.BlockSpec((TM_PROJ, D_MODEL), lambda i, j: (i, 0)),
            pl.BlockSpec((D_MODEL, TN_PROJ), lambda i, j: (0, j)),
        ],
        out_specs=pl.BlockSpec((TM_PROJ, TN_PROJ), lambda i, j: (i, j)),
        out_shape=jax.ShapeDtypeStruct((m, n), BF16),
        compiler_params=pltpu.CompilerParams(
            dimension_semantics=("parallel", "arbitrary"), vmem_limit_bytes=VMEM_LIMIT),
        name="gate_proj",
    )(h, w_gate)


def _latent_kernel(h_ref, wl_ref, gq_ref, gkv_ref, wq_ref, wkv_ref, cos_ref, sin_ref,
                   mq_ref, mk_ref, qn_ref, qr_ref, kv_ref, kr_ref, *, q_scale):
    lat = jnp.dot(h_ref[...], wl_ref[...], preferred_element_type=F32)
    cq = lat[:, :MLA_Q_LORA]
    ckv = lat[:, MLA_Q_LORA:MLA_Q_LORA + MLA_KV_LORA]
    ka = lat[:, MLA_Q_LORA + MLA_KV_LORA:MLA_Q_LORA + MLA_KV_LORA + LANE]
    kb = lat[:, MLA_Q_LORA + MLA_KV_LORA + LANE:]
    cos = cos_ref[...]
    sin = sin_ref[...]
    mq = mq_ref[...]

    cqn = (cq * _rms_scale(cq) * gq_ref[...]).astype(BF16)
    qall = jnp.dot(cqn, wq_ref[...], preferred_element_type=F32) * q_scale
    nope_w = MLA_HEADS * MLA_NOPE
    qn_ref[...] = qall[:, :nope_w].astype(BF16)
    for hh in range(MLA_HEADS):
        a = qall[:, nope_w + hh * LANE:nope_w + (hh + 1) * LANE]
        b = qall[:, 2 * nope_w + hh * LANE:2 * nope_w + (hh + 1) * LANE]
        qr_ref[:, hh * LANE:(hh + 1) * LANE] = (a * cos + b * sin + mq).astype(BF16)

    ckvn = (ckv * _rms_scale(ckv) * gkv_ref[...]).astype(BF16)
    kv_ref[...] = jnp.dot(ckvn, wkv_ref[...], preferred_element_type=F32).astype(BF16)
    kr_ref[...] = (ka * cos + kb * sin + mk_ref[...]).astype(BF16)


def _latent(h, w_lat, gq, gkv, w_q2, w_kv2, pos_tables, seq):
    m = h.shape[0]
    seq_tiles = seq // TM_LAT
    const = lambda i: (0, 0)
    row = lambda i: (i, 0)
    pos = lambda i: (i % seq_tiles, 0)
    q_scale = (MLA_NOPE + MLA_ROPE) ** -0.5 * LOG2E
    return pl.pallas_call(
        functools.partial(_latent_kernel, q_scale=q_scale),
        grid=(m // TM_LAT,),
        in_specs=[
            pl.BlockSpec((TM_LAT, D_MODEL), row),
            pl.BlockSpec(w_lat.shape, const),
            pl.BlockSpec((1, MLA_Q_LORA), const),
            pl.BlockSpec((1, MLA_KV_LORA), const),
            pl.BlockSpec(w_q2.shape, const),
            pl.BlockSpec(w_kv2.shape, const),
        ] + [pl.BlockSpec((TM_LAT, LANE), pos)] * len(pos_tables),
        out_specs=[
            pl.BlockSpec((TM_LAT, MLA_HEADS * LANE), row),
            pl.BlockSpec((TM_LAT, MLA_HEADS * LANE), row),
            pl.BlockSpec((TM_LAT, w_kv2.shape[1]), row),
            pl.BlockSpec((TM_LAT, LANE), row),
        ],
        out_shape=[
            jax.ShapeDtypeStruct((m, MLA_HEADS * LANE), BF16),
            jax.ShapeDtypeStruct((m, MLA_HEADS * LANE), BF16),
            jax.ShapeDtypeStruct((m, w_kv2.shape[1]), BF16),
            jax.ShapeDtypeStruct((m, LANE), BF16),
        ],
        compiler_params=pltpu.CompilerParams(
            dimension_semantics=("parallel",), vmem_limit_bytes=VMEM_LIMIT),
        name="latent_proj",
    )(h, w_lat, gq, gkv, w_q2, w_kv2, *pos_tables)


def _attn_pipeline(first, advance, is_first, is_last, trip_plan, heads, qk_operands, vt_tile,
                   bias_tile, col_bias, finalize, s_buf, mt_buf, m_ref, acc_ref,
                   last_steps=None):
    def qk_stage(q, t, slot):
        for hh in heads:
            kt, qt = qk_operands(hh, q, t)
            s = jnp.dot(kt, qt, preferred_element_type=F32)
            bias = bias_tile(hh, q, t)
            if bias is not None:
                s = s + bias
            s_buf[hh, slot] = s
            mt_buf[hh, slot] = jnp.max(s, axis=0, keepdims=True)

    def sm_pv_stage(q, t, s_slot):
        for hh in heads:
            s = s_buf[hh, s_slot]
            mt = mt_buf[hh, s_slot]
            cb = col_bias(hh)
            if cb is not None:
                mt = mt + cb
            m_old = jnp.where(is_first(q, t), NEG, m_ref[hh, q])
            m_new = jnp.maximum(m_old, mt)
            m_ref[hh, q] = m_new
            shift = m_new if cb is None else m_new - cb
            p = jnp.exp2(s - shift).astype(BF16)
            pv = jnp.dot(vt_tile(hh, t), p, preferred_element_type=F32)
            acc_ref[hh, q] = jnp.exp2(m_old - m_new) * acc_ref[hh, q] + pv

    ahead = [first]
    for _ in range(QK_LEAD):
        ahead.append(advance(*ahead[-1]))
    for i in range(QK_LEAD):
        qk_stage(*ahead[i], i)

    def step(carry, j):
        lead, cur = carry[:QK_LEAD], carry[QK_LEAD]
        qk_stage(*lead[0], (j + QK_LEAD) % S_SLOTS)
        sm_pv_stage(*cur, j % S_SLOTS)
        return (advance(*lead[0]),) + lead, (is_last and is_last(*cur), cur[0])

    assert all(steps % S_SLOTS == 0 for _, steps in trip_plan)

    def make_body(steps):
        def body(_, carry):
            done = []
            for j in range(steps):
                q = carry[QK_LEAD][0]
                carry, d = step(carry, j)
                done.append(d)
                if last_steps is not None and last_steps(j):
                    for hh in heads:
                        finalize(hh, q, acc_ref[hh, q])
            for ended, q in done:
                if ended is None:
                    continue

                @pl.when(ended)
                def _():
                    for hh in heads:
                        finalize(hh, q, acc_ref[hh, q])
            return carry
        return body

    carry = tuple(reversed(ahead))
    for n_trips, steps in trip_plan:
        carry = lax.fori_loop(0, n_trips, make_body(steps), carry)


def _init_attn_state(m_ref, acc_ref):
    m_ref[...] = jnp.full(m_ref.shape, NEG, F32)
    acc_ref[...] = jnp.zeros(acc_ref.shape, F32)


def _attn_scratch(nh, n_kv, n_slots, tk, n):
    return [pltpu.VMEM((nh, n_kv, DV_AUG, tk), BF16),
            pltpu.VMEM((nh, S_SLOTS, tk, n), F32),
            pltpu.VMEM((nh, S_SLOTS, 1, n), F32),
            pltpu.VMEM((nh, n_slots, 1, n), F32),
            pltpu.VMEM((nh, n_slots, DV_AUG, n), F32)]


def _fill_vt(vt_buf, hh, t, v):
    dv, tk = v.shape[1], v.shape[0]
    vt_buf[hh, t, :dv, :] = v.T
    row = lax.broadcasted_iota(jnp.int32, (DV_AUG - dv, tk), 0)
    vt_buf[hh, t, dv:, :] = jnp.where(row == 0, 1.0, 0.0).astype(BF16)


def _diff_attn_kernel(lam_ref, q_ref, k_ref, v_ref, gate_ref, tab_ref, cf_ref, g_ref, o_ref,
                      qz_buf, vt_buf, *scratch, out_scale):
    tq = TQ_DIFF
    hd = 2 * DIFF_HEAD_DIM
    assert hd == MLA_V_DIM
    n_q = q_ref.shape[1] // tq
    heads = range(NH_DIFF)
    lam = lam_ref[0, 0]

    def rows_of(i):
        return pl.ds(pl.multiple_of(i * tq, tq), tq)

    def cols_of(hh):
        return slice(hh * hd, (hh + 1) * hd)

    feat = lax.broadcasted_iota(jnp.int32, (hd, tq), 0)

    def prep(qi, carry):
        for hh in heads:
            qb = q_ref[0, rows_of(qi), cols_of(hh)].T
            zero = jnp.zeros_like(qb)
            qz_buf[hh, qi, :, :tq] = jnp.where(feat < DIFF_HEAD_DIM, qb, zero)
            qz_buf[hh, qi, :, tq:] = jnp.where(feat >= DIFF_HEAD_DIM, qb, zero)
            _fill_vt(vt_buf, hh, qi, v_ref[0, rows_of(qi), cols_of(hh)])
        return carry
    lax.fori_loop(0, n_q, prep, 0)
    idle = n_q
    for hh in heads:
        qz_buf[hh, idle] = jnp.zeros(qz_buf.shape[2:], BF16)

    def qk_operands(hh, qi, t):
        return k_ref[0, rows_of(t), cols_of(hh)], qz_buf[hh, qi]

    def vt_tile(hh, t):
        return vt_buf[hh, t]

    def far_advance(q, t):
        nxt = t >= q - 2
        q2 = jnp.where(nxt, q + 1, q)
        return jnp.where(q2 >= n_q, idle, q2), jnp.where(nxt, 0, t + 1)

    def near_advance(q, t):
        nxt = t >= q
        q2 = jnp.where(nxt, q + 1, q)
        return (jnp.where(q2 >= n_q, idle, q2),
                jnp.where(q2 >= n_q, 0, jnp.where(nxt, q, t + 1)))

    def near_bias(hh, qi, t):
        return tab_ref[hh, jnp.maximum(t - qi + 1, 0)]

    def finalize(hh, qi, acc):
        inv = 1.0 / acc[hd:hd + 1]
        o_t = acc[:hd, :tq] * inv[:, :tq] - lam * (acc[:hd, tq:] * inv[:, tq:])
        o_t = o_t * lax.rsqrt(jnp.mean(o_t * o_t, axis=0, keepdims=True) + EPS)
        o = o_t.T * g_ref[...] * out_scale
        gate = gate_ref[0, rows_of(qi), cols_of(hh)].astype(F32)
        o_ref[0, rows_of(qi), cols_of(hh)] = (o * gate).astype(BF16)

    assert sum(n * steps for n, steps in TRIPS_DIFF_FAR) >= (n_q - 2) * (n_q - 1) // 2
    assert sum(n * steps for n, steps in TRIPS_DIFF_NEAR) == 2 * n_q
    assert all(steps % 2 == 0 for _, steps in TRIPS_DIFF_NEAR)
    _init_attn_state(*scratch[2:])
    two = jnp.int32(2)
    zero = jnp.int32(0)
    _attn_pipeline((two, zero), far_advance, lambda q, t: t == 0, None,
                   TRIPS_DIFF_FAR, heads, qk_operands, vt_tile, lambda hh, qi, t: None,
                   lambda hh: cf_ref[hh], finalize, *scratch)
    _attn_pipeline((zero, zero), near_advance, lambda q, t: q + t < 2, None, TRIPS_DIFF_NEAR,
                   heads, qk_operands, vt_tile, near_bias, lambda hh: None, finalize, *scratch,
                   last_steps=lambda j: j % 2 == 0)


def _diff_attn(qkv3, gates3, lam, tab, cf, g, lam_init):
    b, s, _ = qkv3.shape
    tq = TQ_DIFF
    hd = 2 * DIFF_HEAD_DIM
    nh = NH_DIFF
    blk = nh * hd
    kcol, vcol = DIFF_WIDTH // blk, 2 * DIFF_WIDTH // blk
    seq_blk = lambda off: pl.BlockSpec((1, s, blk), lambda bi, h: (bi, 0, off + h))
    return pl.pallas_call(
        functools.partial(_diff_attn_kernel, out_scale=1.0 - lam_init),
        grid=(b, DIFF_HEADS // nh),
        in_specs=[
            pl.BlockSpec(memory_space=pltpu.SMEM),
            seq_blk(0), seq_blk(kcol), seq_blk(vcol), seq_blk(0),
            pl.BlockSpec((nh, 2, tq, 2 * tq), lambda bi, h: (h, 0, 0, 0)),
            pl.BlockSpec((nh, 1, 2 * tq), lambda bi, h: (h, 0, 0)),
            pl.BlockSpec((1, hd), lambda bi, h: (0, 0)),
        ],
        out_specs=seq_blk(0),
        out_shape=jax.ShapeDtypeStruct((b, s, DIFF_WIDTH), BF16),
        scratch_shapes=[pltpu.VMEM((nh, s // tq + 1, hd, 2 * tq), BF16)]
        + _attn_scratch(nh, s // tq, s // tq + 1, tq, 2 * tq),
        compiler_params=pltpu.CompilerParams(
            dimension_semantics=("parallel", "arbitrary"), vmem_limit_bytes=VMEM_LIMIT),
        name="diff_attn",
    )(lam, qkv3, qkv3, qkv3, gates3, tab, cf, g)


def _mla_attn_kernel(qn_ref, qr_ref, kv_ref, kr_ref, gate_ref, o_ref, qt_buf, vt_buf, *scratch):
    tq = TQ_MLA
    n_q = qn_ref.shape[1] // tq
    heads = range(NH_MLA)

    def rows_of(i):
        return pl.ds(pl.multiple_of(i * tq, tq), tq)

    def cols_of(hh):
        return slice(hh * LANE, (hh + 1) * LANE)

    def prep(qi, carry):
        for hh in heads:
            qt_buf[hh, qi, :LANE, :] = qn_ref[0, rows_of(qi), cols_of(hh)].T
            qt_buf[hh, qi, LANE:, :] = qr_ref[0, rows_of(qi), cols_of(hh)].T
            _fill_vt(vt_buf, hh, qi, kv_ref[0, rows_of(qi), cols_of(2 * hh + 1)])
        return carry
    lax.fori_loop(0, n_q, prep, 0)

    def qk_operands(hh, qi, t):
        kt = jnp.concatenate([kv_ref[0, rows_of(t), cols_of(2 * hh)], kr_ref[0, rows_of(t), :]],
                             axis=1)
        return kt, qt_buf[hh, qi]

    def vt_tile(hh, t):
        return vt_buf[hh, t]

    def finalize(hh, qi, acc):
        dv = MLA_V_DIM
        o = (acc[:dv] * (1.0 / acc[dv:dv + 1])).T
        gate = gate_ref[0, rows_of(qi), cols_of(hh)].astype(F32)
        o_ref[0, rows_of(qi), cols_of(hh)] = (o * gate).astype(BF16)

    last = n_q - 1

    def far_advance(q, t):
        nxt = t >= q - 1
        stay = jnp.logical_and(nxt, q >= last)
        return (jnp.where(jnp.logical_and(nxt, q < last), q + 1, q),
                jnp.where(stay, t, jnp.where(nxt, 0, t + 1)))

    def diag_advance(q, t):
        q2 = jnp.minimum(q + 1, last)
        return q2, q2

    no_bias = lambda hh, qi, t: None
    no_col = lambda hh: None
    assert sum(n * steps for n, steps in TRIPS_MLA_FAR) == n_q * (n_q - 1) // 2
    assert sum(n * steps for n, steps in TRIPS_MLA_DIAG) == n_q
    _init_attn_state(*scratch[2:])
    zero = jnp.int32(0)
    _attn_pipeline((jnp.int32(1), zero), far_advance, lambda q, t: t == 0, None, TRIPS_MLA_FAR,
                   heads, qk_operands, vt_tile, no_bias, no_col, finalize, *scratch)
    _attn_pipeline((zero, zero), diag_advance, lambda q, t: q == 0, None, TRIPS_MLA_DIAG,
                   heads, qk_operands, vt_tile, no_bias, no_col, finalize, *scratch,
                   last_steps=lambda j: True)


def _mla_attn(qn3, qr3, kv3, kr3, gates3):
    b, s, _ = qn3.shape
    tq = TQ_MLA
    nh = NH_MLA
    blk = nh * LANE
    gcol = DIFF_WIDTH // blk
    seq_blk = lambda w, off: pl.BlockSpec((1, s, w), lambda bi, h: (bi, 0, off + h))
    return pl.pallas_call(
        _mla_attn_kernel,
        grid=(b, MLA_HEADS // nh),
        in_specs=[
            seq_blk(blk, 0), seq_blk(blk, 0), seq_blk(2 * blk, 0),
            pl.BlockSpec((1, s, LANE), lambda bi, h: (bi, 0, 0)),
            seq_blk(blk, gcol),
        ],
        out_specs=seq_blk(blk, 0),
        out_shape=jax.ShapeDtypeStruct((b, s, MLA_WIDTH), BF16),
        scratch_shapes=[pltpu.VMEM((nh, s // tq, 2 * LANE, tq), BF16)]
        + _attn_scratch(nh, s // tq, s // tq, tq, tq),
        compiler_params=pltpu.CompilerParams(
            dimension_semantics=("parallel", "arbitrary"), vmem_limit_bytes=VMEM_LIMIT),
        name="mla_attn",
    )(qn3, qr3, kv3, kr3, gates3)


def _out_proj_kernel(x_ref, ya_ref, yb_ref, w_ref, gf_ref, o_ref, *, final):
    acc = jnp.dot(ya_ref[...], w_ref[:DIFF_WIDTH, :], preferred_element_type=F32)
    acc = acc + jnp.dot(yb_ref[...], w_ref[DIFF_WIDTH:, :], preferred_element_type=F32)
    xn = x_ref[...] + acc
    if final:
        xn = xn * _rms_scale(xn) * gf_ref[...]
    o_ref[...] = xn


def _out_proj(x2, ya, yb, w_out, gf, final):
    m = x2.shape[0]
    row = lambda i: (i, 0)
    const = lambda i: (0, 0)
    return pl.pallas_call(
        functools.partial(_out_proj_kernel, final=final),
        grid=(m // TM_OUT,),
        in_specs=[
            pl.BlockSpec((TM_OUT, D_MODEL), row),
            pl.BlockSpec((TM_OUT, DIFF_WIDTH), row),
            pl.BlockSpec((TM_OUT, MLA_WIDTH), row),
            pl.BlockSpec((MIX_WIDTH, D_MODEL), const),
            pl.BlockSpec((1, D_MODEL), const),
        ],
        out_specs=pl.BlockSpec((TM_OUT, D_MODEL), row),
        out_shape=jax.ShapeDtypeStruct((m, D_MODEL), F32),
        compiler_params=pltpu.CompilerParams(
            dimension_semantics=("parallel",), vmem_limit_bytes=VMEM_LIMIT),
        name="out_proj",
    )(x2, ya, yb, w_out, gf)


def _rel_bucket(rel):
    nb = REL_BUCKETS // 2
    max_exact = nb // 2
    ret = (rel > 0).astype(jnp.int32) * nb
    n = jnp.abs(rel)
    nf = jnp.maximum(n, 1).astype(F32)
    large = max_exact + (jnp.log(nf / max_exact) / math.log(REL_MAX_DIST / max_exact)
                         * (nb - max_exact)).astype(jnp.int32)
    large = jnp.minimum(large, nb - 1)
    return ret + jnp.where(n < max_exact, n, large)


def _bias_tables(rel_bias):
    tq = TQ_DIFF
    kk = jnp.arange(2 * tq, dtype=jnp.int32)[:, None]
    qq = jnp.arange(tq, dtype=jnp.int32)[None, :]
    span = 3 * tq
    rel = jnp.arange(span, dtype=jnp.int32) - (2 * tq - 1)
    vec = (rel_bias[_rel_bucket(rel)].astype(F32) * LOG2E).T
    m = jnp.tile(vec, (1, tq))[:, :tq * (span - 1)].reshape(-1, tq, span - 1)
    bias = m[:, :, tq - 1:].transpose(0, 2, 1)
    allowed = (kk < tq) | (((kk - tq) // CHUNK) <= (qq // CHUNK))
    bias = jnp.where(allowed[None], bias, NEG)
    near = bias.reshape(DIFF_HEADS, 2, 2, tq, tq).transpose(0, 2, 3, 1, 4)
    near = near.reshape(DIFF_HEADS, 2, tq, 2 * tq)
    far = rel_bias[_rel_bucket(jnp.asarray(-2 * REL_MAX_DIST, jnp.int32))].astype(F32) * LOG2E
    far = jnp.repeat(far.reshape(DIFF_HEADS, 2), tq, axis=1)
    return near, far.reshape(DIFF_HEADS, 1, 2 * tq)


def _pos_tables(seq):
    pos = jnp.arange(seq, dtype=F32)
    inv_freq = ROPE_BASE ** (-jnp.arange(0, MLA_ROPE, 2, dtype=F32) / MLA_ROPE)
    ang = pos[:, None] * inv_freq[None, :]
    cos, sin = jnp.cos(ang), jnp.sin(ang)
    pad = jnp.zeros((seq, LANE - MLA_ROPE), F32)
    n_chunks = seq // CHUNK
    assert n_chunks <= LANE - MLA_ROPE
    chunk = (jnp.arange(seq, dtype=jnp.int32) // CHUNK)[:, None]
    cid = jnp.arange(LANE, dtype=jnp.int32)[None, :] - MLA_ROPE
    valid = (cid >= 0) & (cid < n_chunks)
    mq = jnp.where(valid & (cid > chunk), NEG, 0.0).astype(F32)
    mk = jnp.where(valid & (cid == chunk), 1.0, 0.0).astype(F32)
    return (jnp.concatenate([cos, cos, pad], axis=1),
            jnp.concatenate([-sin, sin, pad], axis=1), mq, mk)


def _swap_halves(w):
    half = w.shape[-1] // 2
    return jnp.concatenate([w[..., half:], w[..., :half]], axis=-1)


def _pad_lanes(w):
    return jnp.pad(w, [(0, 0)] * (w.ndim - 1) + [(0, LANE - w.shape[-1])])


def kernel(x, norm_g, w_in, diff_lambda, diff_subln_g, mla_q_norm_g, mla_w_q_b,
           mla_kv_norm_g, mla_w_kv_b, w_out, rel_bias, final_norm_g):
    b, s, d = x.shape
    assert d == D_MODEL and s % TQ_MLA == 0 and s >= 2 * TQ_MLA and (b * s) % TM_PROJ == 0
    m = b * s
    pos_tables = _pos_tables(s)
    tab, cf = _bias_tables(rel_bias)
    colscale = jnp.concatenate([
        jnp.full((1, DIFF_WIDTH), DIFF_HEAD_DIM ** -0.5 * LOG2E, F32),
        jnp.ones((1, QKV_WIDTH - DIFF_WIDTH), F32)], axis=1)
    gf = final_norm_g.reshape(1, D_MODEL)

    x2 = x.reshape(m, d)
    for l in range(DEPTH):
        wl = w_in[l]
        w_qkv = wl[:, :QKV_WIDTH].astype(BF16)
        w_gate = wl[:, GATE_OFF:].astype(BF16)
        w_kr = wl[:, GATE_OFF - MLA_ROPE:GATE_OFF]
        w_lat = jnp.concatenate([wl[:, LAT_OFF:GATE_OFF - MLA_ROPE], _pad_lanes(w_kr),
                                 _pad_lanes(_swap_halves(w_kr))], axis=1).astype(BF16)
        wq = mla_w_q_b[l].reshape(MLA_Q_LORA, MLA_HEADS, MLA_NOPE + MLA_ROPE)
        wq_rope = wq[:, :, MLA_NOPE:]
        w_q2 = jnp.concatenate([
            wq[:, :, :MLA_NOPE].reshape(MLA_Q_LORA, -1),
            _pad_lanes(wq_rope).reshape(MLA_Q_LORA, -1),
            _pad_lanes(_swap_halves(wq_rope)).reshape(MLA_Q_LORA, -1)], axis=1).astype(BF16)
        w_kv2 = mla_w_kv_b[l].astype(BF16)
        g = norm_g[l].reshape(1, D_MODEL)

        lam_init = 0.8 - 0.6 * math.exp(-0.3 * l)
        lp = diff_lambda[l].astype(F32)
        lam = jnp.exp(jnp.sum(lp[0] * lp[1])) - jnp.exp(jnp.sum(lp[2] * lp[3])) + lam_init

        qkv, h = _qkv_proj(x2, g, w_qkv, colscale)
        gates3 = _gate_proj(h, w_gate).reshape(b, s, MIX_WIDTH)
        qn, qr, kv, kr = _latent(h, w_lat, mla_q_norm_g[l].reshape(1, -1),
                                 mla_kv_norm_g[l].reshape(1, -1), w_q2, w_kv2, pos_tables, s)
        ya = _diff_attn(qkv.reshape(b, s, QKV_WIDTH), gates3, lam.reshape(1, 1), tab, cf,
                        diff_subln_g[l].reshape(1, -1), lam_init)
        yb = _mla_attn(qn.reshape(b, s, -1), qr.reshape(b, s, -1), kv.reshape(b, s, -1),
                       kr.reshape(b, s, -1), gates3)
        x2 = _out_proj(x2, ya.reshape(m, -1), yb.reshape(m, -1), w_out[l].astype(BF16), gf,
                       final=(l == DEPTH - 1))
    return x2.reshape(b, s, d)
```

```python
import functools
import math

import jax
import jax.numpy as jnp
from jax import lax
from jax.experimental import pallas as pl
from jax.experimental.pallas import tpu as pltpu

D_MODEL = 2048
DEPTH = 2
CHUNK = 64
MIX_WIDTH = D_MODEL
DIFF_WIDTH = MIX_WIDTH // 2
MLA_WIDTH = MIX_WIDTH - DIFF_WIDTH
DIFF_HEAD_DIM = 64
DIFF_HEADS = DIFF_WIDTH // (2 * DIFF_HEAD_DIM)
MLA_V_DIM = 128
MLA_HEADS = MLA_WIDTH // MLA_V_DIM
MLA_NOPE = 128
MLA_ROPE = 64
MLA_Q_LORA = 512
MLA_KV_LORA = 256
ROPE_BASE = 10000.0
REL_BUCKETS = 32
REL_MAX_DIST = 128
EPS = 1e-6
NEG = -1e30
QKV_WIDTH = 3 * DIFF_WIDTH
LAT_OFF = QKV_WIDTH
GATE_OFF = LAT_OFF + MLA_Q_LORA + MLA_KV_LORA + MLA_ROPE

LOG2E = math.log2(math.e)
LANE = 128
VMEM_LIMIT = 56 * 1024 * 1024

BF16 = jnp.bfloat16
F32 = jnp.float32

TM_PROJ = 1024
TN_PROJ = 1024
RC_NORM = 256
TM_LAT = 512
TM_OUT = 512
TQ_DIFF = 256
TQ_MLA = 512
NH_DIFF = 1
NH_MLA = 1
TRIPS_DIFF_FAR = ((9, 12),)
TRIPS_DIFF_NEAR = ((2, 16),)
TRIPS_MLA_FAR = ((2, 12), (1, 4))
TRIPS_MLA_DIAG = ((2, 4),)
QK_LEAD = 2
S_SLOTS = 4

DV_AUG = MLA_V_DIM + 16


def _rms_scale(xf):
    return lax.rsqrt(jnp.mean(xf * xf, axis=-1, keepdims=True) + EPS)


def _qkv_proj_kernel(x_ref, g_ref, w_ref, cs_ref, o_ref, h_ref):
    @pl.when(pl.program_id(1) == 0)
    def _():
        def body(c, carry):
            rows = pl.ds(pl.multiple_of(c * RC_NORM, RC_NORM), RC_NORM)
            xf = x_ref[rows, :]
            h_ref[rows, :] = (xf * _rms_scale(xf) * g_ref[...]).astype(BF16)
            return carry
        lax.fori_loop(0, TM_PROJ // RC_NORM, body, 0)

    acc = jnp.dot(h_ref[...], w_ref[...], preferred_element_type=F32)
    o_ref[...] = (acc * cs_ref[...]).astype(BF16)


def _qkv_proj(x2, g, w_qkv, colscale):
    m = x2.shape[0]
    n = w_qkv.shape[1]
    return pl.pallas_call(
        _qkv_proj_kernel,
        grid=(m // TM_PROJ, n // TN_PROJ),
        in_specs=[
            pl.BlockSpec((TM_PROJ, D_MODEL), lambda i, j: (i, 0)),
            pl.BlockSpec((1, D_MODEL), lambda i, j: (0, 0)),
            pl.BlockSpec((D_MODEL, TN_PROJ), lambda i, j: (0, j)),
            pl.BlockSpec((1, TN_PROJ), lambda i, j: (0, j)),
        ],
        out_specs=[pl.BlockSpec((TM_PROJ, TN_PROJ), lambda i, j: (i, j)),
                   pl.BlockSpec((TM_PROJ, D_MODEL), lambda i, j: (i, 0))],
        out_shape=[jax.ShapeDtypeStruct((m, n), BF16),
                   jax.ShapeDtypeStruct((m, D_MODEL), BF16)],
        compiler_params=pltpu.CompilerParams(
            dimension_semantics=("parallel", "arbitrary"), vmem_limit_bytes=VMEM_LIMIT),
        name="qkv_proj",
    )(x2, g, w_qkv, colscale)


def _gate_proj_kernel(h_ref, w_ref, o_ref):
    acc = jnp.dot(h_ref[...], w_ref[...], preferred_element_type=F32)
    o_ref[...] = (acc * jax.nn.sigmoid(acc)).astype(BF16)


def _gate_proj(h, w_gate):
    m = h.shape[0]
    n = w_gate.shape[1]
    return pl.pallas_call(
        _gate_proj_kernel,
        grid=(m // TM_PROJ, n // TN_PROJ),
        in_specs=[
            pl.BlockSpec((TM_PROJ, D_MODEL), lambda i, j: (i, 0)),
            pl.BlockSpec((D_MODEL, TN_PROJ), lambda i, j: (0, j)),
        ],
        out_specs=pl.BlockSpec((TM_PROJ, TN_PROJ), lambda i, j: (i, j)),
        out_shape=jax.ShapeDtypeStruct((m, n), BF16),
        compiler_params=pltpu.CompilerParams(
            dimension_semantics=("parallel", "arbitrary"), vmem_limit_bytes=VMEM_LIMIT),
        name="gate_proj",
    )(h, w_gate)


def _latent_kernel(h_ref, wl_ref, gq_ref, gkv_ref, wq_ref, wkv_ref, cos_ref, sin_ref,
                   mq_ref, mk_ref, qn_ref, qr_ref, kv_ref, kr_ref, *, q_scale):
    lat = jnp.dot(h_ref[...], wl_ref[...], preferred_element_type=F32)
    cq = lat[:, :MLA_Q_LORA]
    ckv = lat[:, MLA_Q_LORA:MLA_Q_LORA + MLA_KV_LORA]
    ka = lat[:, MLA_Q_LORA + MLA_KV_LORA:MLA_Q_LORA + MLA_KV_LORA + LANE]
    kb = lat[:, MLA_Q_LORA + MLA_KV_LORA + LANE:]
    cos = cos_ref[...]
    sin = sin_ref[...]
    mq = mq_ref[...]

    cqn = (cq * _rms_scale(cq) * gq_ref[...]).astype(BF16)
    qall = jnp.dot(cqn, wq_ref[...], preferred_element_type=F32) * q_scale
    nope_w = MLA_HEADS * MLA_NOPE
    qn_ref[...] = qall[:, :nope_w].astype(BF16)
    for hh in range(MLA_HEADS):
        a = qall[:, nope_w + hh * LANE:nope_w + (hh + 1) * LANE]
        b = qall[:, 2 * nope_w + hh * LANE:2 * nope_w + (hh + 1) * LANE]
        qr_ref[:, hh * LANE:(hh + 1) * LANE] = (a * cos + b * sin + mq).astype(BF16)

    ckvn = (ckv * _rms_scale(ckv) * gkv_ref[...]).astype(BF16)
    kv_ref[...] = jnp.dot(ckvn, wkv_ref[...], preferred_element_type=F32).astype(BF16)
    kr_ref[...] = (ka * cos + kb * sin + mk_ref[...]).astype(BF16)


def _latent(h, w_lat, gq, gkv, w_q2, w_kv2, pos_tables, seq):
    m = h.shape[0]
    seq_tiles = seq // TM_LAT
    const = lambda i: (0, 0)
    row = lambda i: (i, 0)
    pos = lambda i: (i % seq_tiles, 0)
    q_scale = (MLA_NOPE + MLA_ROPE) ** -0.5 * LOG2E
    return pl.pallas_call(
        functools.partial(_latent_kernel, q_scale=q_scale),
        grid=(m // TM_LAT,),
        in_specs=[
            pl.BlockSpec((TM_LAT, D_MODEL), row),
            pl.BlockSpec(w_lat.shape, const),
            pl.BlockSpec((1, MLA_Q_LORA), const),
            pl.BlockSpec((1, MLA_KV_LORA), const),
            pl.BlockSpec(w_q2.shape, const),
            pl.BlockSpec(w_kv2.shape, const),
        ] + [pl.BlockSpec((TM_LAT, LANE), pos)] * len(pos_tables),
        out_specs=[
            pl.BlockSpec((TM_LAT, MLA_HEADS * LANE), row),
            pl.BlockSpec((TM_LAT, MLA_HEADS * LANE), row),
            pl.BlockSpec((TM_LAT, w_kv2.shape[1]), row),
            pl.BlockSpec((TM_LAT, LANE), row),
        ],
        out_shape=[
            jax.ShapeDtypeStruct((m, MLA_HEADS * LANE), BF16),
            jax.ShapeDtypeStruct((m, MLA_HEADS * LANE), BF16),
            jax.ShapeDtypeStruct((m, w_kv2.shape[1]), BF16),
            jax.ShapeDtypeStruct((m, LANE), BF16),
        ],
        compiler_params=pltpu.CompilerParams(
            dimension_semantics=("parallel",), vmem_limit_bytes=VMEM_LIMIT),
        name="latent_proj",
    )(h, w_lat, gq, gkv, w_q2, w_kv2, *pos_tables)


def _attn_pipeline(first, advance, is_first, trip_plan, heads, qk_operands, vt_tile,
                   bias_tile, col_bias, finalize, s_buf, mt_buf, m_ref, acc_ref,
                   last_steps=None):
    def qk_stage(q, t, slot):
        for hh in heads:
            kt, qt = qk_operands(hh, q, t)
            s = jnp.dot(kt, qt, preferred_element_type=F32)
            bias = bias_tile(hh, q, t)
            if bias is not None:
                s = s + bias
            s_buf[hh, slot] = s
            mt_buf[hh, slot] = jnp.max(s, axis=0, keepdims=True)

    def sm_pv_stage(q, t, s_slot):
        for hh in heads:
            s = s_buf[hh, s_slot]
            mt = mt_buf[hh, s_slot]
            cb = col_bias(hh)
            if cb is not None:
                mt = mt + cb
            m_old = jnp.where(is_first(q, t), NEG, m_ref[hh, q])
            m_new = jnp.maximum(m_old, mt)
            m_ref[hh, q] = m_new
            shift = m_new if cb is None else m_new - cb
            p = jnp.exp2(s - shift).astype(BF16)
            pv = jnp.dot(vt_tile(hh, t), p, preferred_element_type=F32)
            acc_ref[hh, q] = jnp.exp2(m_old - m_new) * acc_ref[hh, q] + pv

    ahead = [first]
    for _ in range(QK_LEAD):
        ahead.append(advance(*ahead[-1]))
    for i in range(QK_LEAD):
        qk_stage(*ahead[i], i)

    def step(carry, j):
        lead, cur = carry[:QK_LEAD], carry[QK_LEAD]
        qk_stage(*lead[0], (j + QK_LEAD) % S_SLOTS)
        sm_pv_stage(*cur, j % S_SLOTS)
        return (advance(*lead[0]),) + lead

    assert all(steps % S_SLOTS == 0 for _, steps in trip_plan)

    def make_body(steps):
        def body(_, carry):
            for j in range(steps):
                q = carry[QK_LEAD][0]
                carry = step(carry, j)
                if last_steps is not None and last_steps(j):
                    for hh in heads:
                        finalize(hh, q, acc_ref[hh, q])
            return carry
        return body

    carry = tuple(reversed(ahead))
    for n_trips, steps in trip_plan:
        carry = lax.fori_loop(0, n_trips, make_body(steps), carry)


def _init_attn_state(m_ref, acc_ref):
    m_ref[...] = jnp.full(m_ref.shape, NEG, F32)
    acc_ref[...] = jnp.zeros(acc_ref.shape, F32)


def _attn_scratch(nh, n_kv, n_slots, tk, n):
    return [pltpu.VMEM((nh, n_kv, DV_AUG, tk), BF16),
            pltpu.VMEM((nh, S_SLOTS, tk, n), F32),
            pltpu.VMEM((nh, S_SLOTS, 1, n), F32),
            pltpu.VMEM((nh, n_slots, 1, n), F32),
            pltpu.VMEM((nh, n_slots, DV_AUG, n), F32)]


def _fill_vt(vt_buf, hh, t, v):
    dv, tk = v.shape[1], v.shape[0]
    vt_buf[hh, t, :dv, :] = v.T
    row = lax.broadcasted_iota(jnp.int32, (DV_AUG - dv, tk), 0)
    vt_buf[hh, t, dv:, :] = jnp.where(row == 0, 1.0, 0.0).astype(BF16)


def _diff_attn_kernel(lam_ref, q_ref, k_ref, v_ref, gate_ref, tab_ref, cf_ref, g_ref, o_ref,
                      qz_buf, vt_buf, *scratch, out_scale):
    tq = TQ_DIFF
    hd = 2 * DIFF_HEAD_DIM
    assert hd == MLA_V_DIM
    n_q = q_ref.shape[1] // tq
    heads = range(NH_DIFF)
    lam = lam_ref[0, 0]

    def rows_of(i):
        return pl.ds(pl.multiple_of(i * tq, tq), tq)

    def cols_of(hh):
        return slice(hh * hd, (hh + 1) * hd)

    feat = lax.broadcasted_iota(jnp.int32, (hd, tq), 0)

    def prep(qi, carry):
        for hh in heads:
            qb = q_ref[0, rows_of(qi), cols_of(hh)].T
            zero = jnp.zeros_like(qb)
            qz_buf[hh, qi, :, :tq] = jnp.where(feat < DIFF_HEAD_DIM, qb, zero)
            qz_buf[hh, qi, :, tq:] = jnp.where(feat >= DIFF_HEAD_DIM, qb, zero)
            _fill_vt(vt_buf, hh, qi, v_ref[0, rows_of(qi), cols_of(hh)])
        return carry
    lax.fori_loop(0, n_q, prep, 0)
    idle = n_q
    for hh in heads:
        qz_buf[hh, idle] = jnp.zeros(qz_buf.shape[2:], BF16)

    def qk_operands(hh, qi, t):
        return k_ref[0, rows_of(t), cols_of(hh)], qz_buf[hh, qi]

    def vt_tile(hh, t):
        return vt_buf[hh, t]

    def far_advance(q, t):
        nxt = t >= q - 2
        q2 = jnp.where(nxt, q + 1, q)
        return jnp.where(q2 >= n_q, idle, q2), jnp.where(nxt, 0, t + 1)

    def near_advance(q, t):
        nxt = t >= q
        q2 = jnp.where(nxt, q + 1, q)
        return (jnp.where(q2 >= n_q, idle, q2),
                jnp.where(q2 >= n_q, 0, jnp.where(nxt, q, t + 1)))

    def near_bias(hh, qi, t):
        return tab_ref[hh, jnp.maximum(t - qi + 1, 0)]

    def finalize(hh, qi, acc):
        inv = 1.0 / acc[hd:hd + 1]
        o_t = acc[:hd, :tq] * inv[:, :tq] - lam * (acc[:hd, tq:] * inv[:, tq:])
        o_t = o_t * lax.rsqrt(jnp.mean(o_t * o_t, axis=0, keepdims=True) + EPS)
        o = o_t.T * g_ref[...] * out_scale
        gate = gate_ref[0, rows_of(qi), cols_of(hh)].astype(F32)
        o_ref[0, rows_of(qi), cols_of(hh)] = (o * gate).astype(BF16)

    assert sum(n * steps for n, steps in TRIPS_DIFF_FAR) >= (n_q - 2) * (n_q - 1) // 2
    assert sum(n * steps for n, steps in TRIPS_DIFF_NEAR) == 2 * n_q
    assert all(steps % 2 == 0 for _, steps in TRIPS_DIFF_NEAR)
    _init_attn_state(*scratch[2:])
    two = jnp.int32(2)
    zero = jnp.int32(0)
    _attn_pipeline((two, zero), far_advance, lambda q, t: t == 0,
                   TRIPS_DIFF_FAR, heads, qk_operands, vt_tile, lambda hh, qi, t: None,
                   lambda hh: cf_ref[hh], finalize, *scratch)
    _attn_pipeline((zero, zero), near_advance, lambda q, t: q + t < 2, TRIPS_DIFF_NEAR,
                   heads, qk_operands, vt_tile, near_bias, lambda hh: None, finalize, *scratch,
                   last_steps=lambda j: j % 2 == 0)


def _diff_attn(qkv3, gates3, lam, tab, cf, g, lam_init):
    b, s, _ = qkv3.shape
    tq = TQ_DIFF
    hd = 2 * DIFF_HEAD_DIM
    nh = NH_DIFF
    blk = nh * hd
    kcol, vcol = DIFF_WIDTH // blk, 2 * DIFF_WIDTH // blk
    seq_blk = lambda off: pl.BlockSpec((1, s, blk), lambda bi, h: (bi, 0, off + h))
    return pl.pallas_call(
        functools.partial(_diff_attn_kernel, out_scale=1.0 - lam_init),
        grid=(b, DIFF_HEADS // nh),
        in_specs=[
            pl.BlockSpec(memory_space=pltpu.SMEM),
            seq_blk(0), seq_blk(kcol), seq_blk(vcol), seq_blk(0),
            pl.BlockSpec((nh, 2, tq, 2 * tq), lambda bi, h: (h, 0, 0, 0)),
            pl.BlockSpec((nh, 1, 2 * tq), lambda bi, h: (h, 0, 0)),
            pl.BlockSpec((1, hd), lambda bi, h: (0, 0)),
        ],
        out_specs=seq_blk(0),
        out_shape=jax.ShapeDtypeStruct((b, s, DIFF_WIDTH), BF16),
        scratch_shapes=[pltpu.VMEM((nh, s // tq + 1, hd, 2 * tq), BF16)]
        + _attn_scratch(nh, s // tq, s // tq + 1, tq, 2 * tq),
        compiler_params=pltpu.CompilerParams(
            dimension_semantics=("parallel", "arbitrary"), vmem_limit_bytes=VMEM_LIMIT),
        name="diff_attn",
    )(lam, qkv3, qkv3, qkv3, gates3, tab, cf, g)


def _mla_attn_kernel(qn_ref, qr_ref, kv_ref, kr_ref, gate_ref, o_ref, qt_buf, vt_buf, *scratch):
    tq = TQ_MLA
    n_q = qn_ref.shape[1] // tq
    heads = range(NH_MLA)

    def rows_of(i):
        return pl.ds(pl.multiple_of(i * tq, tq), tq)

    def cols_of(hh):
        return slice(hh * LANE, (hh + 1) * LANE)

    def prep(qi, carry):
        for hh in heads:
            qt_buf[hh, qi, :LANE, :] = qn_ref[0, rows_of(qi), cols_of(hh)].T
            qt_buf[hh, qi, LANE:, :] = qr_ref[0, rows_of(qi), cols_of(hh)].T
            _fill_vt(vt_buf, hh, qi, kv_ref[0, rows_of(qi), cols_of(2 * hh + 1)])
        return carry
    lax.fori_loop(0, n_q, prep, 0)

    def qk_operands(hh, qi, t):
        kt = jnp.concatenate([kv_ref[0, rows_of(t), cols_of(2 * hh)], kr_ref[0, rows_of(t), :]],
                             axis=1)
        return kt, qt_buf[hh, qi]

    def vt_tile(hh, t):
        return vt_buf[hh, t]

    def finalize(hh, qi, acc):
        dv = MLA_V_DIM
        o = (acc[:dv] * (1.0 / acc[dv:dv + 1])).T
        gate = gate_ref[0, rows_of(qi), cols_of(hh)].astype(F32)
        o_ref[0, rows_of(qi), cols_of(hh)] = (o * gate).astype(BF16)

    last = n_q - 1

    def far_advance(q, t):
        nxt = t >= q - 1
        stay = jnp.logical_and(nxt, q >= last)
        return (jnp.where(jnp.logical_and(nxt, q < last), q + 1, q),
                jnp.where(stay, t, jnp.where(nxt, 0, t + 1)))

    def diag_advance(q, t):
        q2 = jnp.minimum(q + 1, last)
        return q2, q2

    no_bias = lambda hh, qi, t: None
    no_col = lambda hh: None
    assert sum(n * steps for n, steps in TRIPS_MLA_FAR) == n_q * (n_q - 1) // 2
    assert sum(n * steps for n, steps in TRIPS_MLA_DIAG) == n_q
    _init_attn_state(*scratch[2:])
    zero = jnp.int32(0)
    _attn_pipeline((jnp.int32(1), zero), far_advance, lambda q, t: t == 0, TRIPS_MLA_FAR,
                   heads, qk_operands, vt_tile, no_bias, no_col, finalize, *scratch)
    _attn_pipeline((zero, zero), diag_advance, lambda q, t: q == 0, TRIPS_MLA_DIAG,
                   heads, qk_operands, vt_tile, no_bias, no_col, finalize, *scratch,
                   last_steps=lambda j: True)


def _mla_attn(qn3, qr3, kv3, kr3, gates3):
    b, s, _ = qn3.shape
    tq = TQ_MLA
    nh = NH_MLA
    blk = nh * LANE
    gcol = DIFF_WIDTH // blk
    seq_blk = lambda w, off: pl.BlockSpec((1, s, w), lambda bi, h: (bi, 0, off + h))
    return pl.pallas_call(
        _mla_attn_kernel,
        grid=(b, MLA_HEADS // nh),
        in_specs=[
            seq_blk(blk, 0), seq_blk(blk, 0), seq_blk(2 * blk, 0),
            pl.BlockSpec((1, s, LANE), lambda bi, h: (bi, 0, 0)),
            seq_blk(blk, gcol),
        ],
        out_specs=seq_blk(blk, 0),
        out_shape=jax.ShapeDtypeStruct((b, s, MLA_WIDTH), BF16),
        scratch_shapes=[pltpu.VMEM((nh, s // tq, 2 * LANE, tq), BF16)]
        + _attn_scratch(nh, s // tq, s // tq, tq, tq),
        compiler_params=pltpu.CompilerParams(
            dimension_semantics=("parallel", "arbitrary"), vmem_limit_bytes=VMEM_LIMIT),
        name="mla_attn",
    )(qn3, qr3, kv3, kr3, gates3)


def _out_proj_kernel(x_ref, ya_ref, yb_ref, w_ref, gf_ref, o_ref, *, final):
    acc = jnp.dot(ya_ref[...], w_ref[:DIFF_WIDTH, :], preferred_element_type=F32)
    acc = acc + jnp.dot(yb_ref[...], w_ref[DIFF_WIDTH:, :], preferred_element_type=F32)
    xn = x_ref[...] + acc
    if final:
        xn = xn * _rms_scale(xn) * gf_ref[...]
    o_ref[...] = xn


def _out_proj(x2, ya, yb, w_out, gf, final):
    m = x2.shape[0]
    row = lambda i: (i, 0)
    const = lambda i: (0, 0)
    return pl.pallas_call(
        functools.partial(_out_proj_kernel, final=final),
        grid=(m // TM_OUT,),
        in_specs=[
            pl.BlockSpec((TM_OUT, D_MODEL), row),
            pl.BlockSpec((TM_OUT, DIFF_WIDTH), row),
            pl.BlockSpec((TM_OUT, MLA_WIDTH), row),
            pl.BlockSpec((MIX_WIDTH, D_MODEL), const),
            pl.BlockSpec((1, D_MODEL), const),
        ],
        out_specs=pl.BlockSpec((TM_OUT, D_MODEL), row),
        out_shape=jax.ShapeDtypeStruct((m, D_MODEL), F32),
        compiler_params=pltpu.CompilerParams(
            dimension_semantics=("parallel",), vmem_limit_bytes=VMEM_LIMIT),
        name="out_proj",
    )(x2, ya, yb, w_out, gf)


def _rel_bucket(rel):
    nb = REL_BUCKETS // 2
    max_exact = nb // 2
    ret = (rel > 0).astype(jnp.int32) * nb
    n = jnp.abs(rel)
    nf = jnp.maximum(n, 1).astype(F32)
    large = max_exact + (jnp.log(nf / max_exact) / math.log(REL_MAX_DIST / max_exact)
                         * (nb - max_exact)).astype(jnp.int32)
    large = jnp.minimum(large, nb - 1)
    return ret + jnp.where(n < max_exact, n, large)


def _bias_tables(rel_bias):
    tq = TQ_DIFF
    kk = jnp.arange(2 * tq, dtype=jnp.int32)[:, None]
    qq = jnp.arange(tq, dtype=jnp.int32)[None, :]
    span = 3 * tq
    rel = jnp.arange(span, dtype=jnp.int32) - (2 * tq - 1)
    vec = (rel_bias[_rel_bucket(rel)].astype(F32) * LOG2E).T
    m = jnp.tile(vec, (1, tq))[:, :tq * (span - 1)].reshape(-1, tq, span - 1)
    bias = m[:, :, tq - 1:].transpose(0, 2, 1)
    allowed = (kk < tq) | (((kk - tq) // CHUNK) <= (qq // CHUNK))
    bias = jnp.where(allowed[None], bias, NEG)
    near = bias.reshape(DIFF_HEADS, 2, 2, tq, tq).transpose(0, 2, 3, 1, 4)
    near = near.reshape(DIFF_HEADS, 2, tq, 2 * tq)
    far = rel_bias[_rel_bucket(jnp.asarray(-2 * REL_MAX_DIST, jnp.int32))].astype(F32) * LOG2E
    far = jnp.repeat(far.reshape(DIFF_HEADS, 2), tq, axis=1)
    return near, far.reshape(DIFF_HEADS, 1, 2 * tq)


def _pos_tables(seq):
    pos = jnp.arange(seq, dtype=F32)
    inv_freq = ROPE_BASE ** (-jnp.arange(0, MLA_ROPE, 2, dtype=F32) / MLA_ROPE)
    ang = pos[:, None] * inv_freq[None, :]
    cos, sin = jnp.cos(ang), jnp.sin(ang)
    pad = jnp.zeros((seq, LANE - MLA_ROPE), F32)
    n_chunks = seq // CHUNK
    assert n_chunks <= LANE - MLA_ROPE
    chunk = (jnp.arange(seq, dtype=jnp.int32) // CHUNK)[:, None]
    cid = jnp.arange(LANE, dtype=jnp.int32)[None, :] - MLA_ROPE
    valid = (cid >= 0) & (cid < n_chunks)
    mq = jnp.where(valid & (cid > chunk), NEG, 0.0).astype(F32)
    mk = jnp.where(valid & (cid == chunk), 1.0, 0.0).astype(F32)
    return (jnp.concatenate([cos, cos, pad], axis=1),
            jnp.concatenate([-sin, sin, pad], axis=1), mq, mk)


def _swap_halves(w):
    half = w.shape[-1] // 2
    return jnp.concatenate([w[..., half:], w[..., :half]], axis=-1)


def _pad_lanes(w):
    return jnp.pad(w, [(0, 0)] * (w.ndim - 1) + [(0, LANE - w.shape[-1])])


def kernel(x, norm_g, w_in, diff_lambda, diff_subln_g, mla_q_norm_g, mla_w_q_b,
           mla_kv_norm_g, mla_w_kv_b, w_out, rel_bias, final_norm_g):
    b, s, d = x.shape
    assert d == D_MODEL and s % TQ_MLA == 0 and s >= 2 * TQ_MLA and (b * s) % TM_PROJ == 0
    m = b * s
    pos_tables = _pos_tables(s)
    tab, cf = _bias_tables(rel_bias)
    colscale = jnp.concatenate([
        jnp.full((1, DIFF_WIDTH), DIFF_HEAD_DIM ** -0.5 * LOG2E, F32),
        jnp.ones((1, QKV_WIDTH - DIFF_WIDTH), F32)], axis=1)
    gf = final_norm_g.reshape(1, D_MODEL)

    x2 = x.reshape(m, d)
    for l in range(DEPTH):
        wl = w_in[l]
        w_qkv = wl[:, :QKV_WIDTH].astype(BF16)
        w_gate = wl[:, GATE_OFF:].astype(BF16)
        w_kr = wl[:, GATE_OFF - MLA_ROPE:GATE_OFF]
        w_lat = jnp.concatenate([wl[:, LAT_OFF:GATE_OFF - MLA_ROPE], _pad_lanes(w_kr),
                                 _pad_lanes(_swap_halves(w_kr))], axis=1).astype(BF16)
        wq = mla_w_q_b[l].reshape(MLA_Q_LORA, MLA_HEADS, MLA_NOPE + MLA_ROPE)
        wq_rope = wq[:, :, MLA_NOPE:]
        w_q2 = jnp.concatenate([
            wq[:, :, :MLA_NOPE].reshape(MLA_Q_LORA, -1),
            _pad_lanes(wq_rope).reshape(MLA_Q_LORA, -1),
            _pad_lanes(_swap_halves(wq_rope)).reshape(MLA_Q_LORA, -1)], axis=1).astype(BF16)
        w_kv2 = mla_w_kv_b[l].astype(BF16)
        g = norm_g[l].reshape(1, D_MODEL)

        lam_init = 0.8 - 0.6 * math.exp(-0.3 * l)
        lp = diff_lambda[l].astype(F32)
        lam = jnp.exp(jnp.sum(lp[0] * lp[1])) - jnp.exp(jnp.sum(lp[2] * lp[3])) + lam_init

        qkv, h = _qkv_proj(x2, g, w_qkv, colscale)
        gates3 = _gate_proj(h, w_gate).reshape(b, s, MIX_WIDTH)
        qn, qr, kv, kr = _latent(h, w_lat, mla_q_norm_g[l].reshape(1, -1),
                                 mla_kv_norm_g[l].reshape(1, -1), w_q2, w_kv2, pos_tables, s)
        ya = _diff_attn(qkv.reshape(b, s, QKV_WIDTH), gates3, lam.reshape(1, 1), tab, cf,
                        diff_subln_g[l].reshape(1, -1), lam_init)
        yb = _mla_attn(qn.reshape(b, s, -1), qr.reshape(b, s, -1), kv.reshape(b, s, -1),
                       kr.reshape(b, s, -1), gates3)
        x2 = _out_proj(x2, ya.reshape(m, -1), yb.reshape(m, -1), w_out[l].astype(BF16), gf,
                       final=(l == DEPTH - 1))
    return x2.reshape(b, s, d)
```
